```python
import math
import jax, jax.numpy as jnp
from jax import lax
import numpy as np

D_MODEL = 1024
BATCH = 2
SEQ = 8192
DEPTH = 4
DEC_BATCH = 128
DEC_SEQ = 4
PAST_LEN = 2048
PAGE_SIZE = 128

N_A_LAYERS = DEPTH // 2
N_B_LAYERS = DEPTH - N_A_LAYERS
N_HEADS = 16
HEAD_DIM = D_MODEL // N_HEADS
CONV_WIDTH = 31
D_FF = 256 * math.ceil(8 * D_MODEL / (3 * 256))
Q_BLOCK = 128
LOGIT_BIAS_INIT = -6.0
RMS_EPS = 1e-6
LN_EPS = 1e-5

kernel_name = 'yoco_conformer_conv_stick_breaking_step'


def rms_norm(x, g):
    xf = x.astype(jnp.float32)
    y = xf * lax.rsqrt(jnp.mean(xf * xf, axis=-1, keepdims=True) + RMS_EPS)
    return (y * g.astype(jnp.float32)).astype(x.dtype)


def layer_norm(x, g, b):
    xf = x.astype(jnp.float32)
    mu = jnp.mean(xf, axis=-1, keepdims=True)
    xc = xf - mu
    y = xc * lax.rsqrt(jnp.mean(xc * xc, axis=-1, keepdims=True) + LN_EPS)
    return (y * g.astype(jnp.float32) + b.astype(jnp.float32)).astype(x.dtype)


def swiglu(h, w_gate, w_up, w_down):
    return (jax.nn.silu(h @ w_gate) * (h @ w_up)) @ w_down


def conformer_conv(h, buf, w_pw1, b_pw1, w_dw, b_dw, ln_g, ln_b, w_pw2, b_pw2):
    a = h @ w_pw1 + b_pw1
    d = a.shape[-1] // 2
    u = a[..., :d] * jax.nn.sigmoid(a[..., d:])
    full = jnp.concatenate([buf.astype(u.dtype), u], axis=1)
    new_buf = full[:, full.shape[1] - (CONV_WIDTH - 1):]
    c = lax.conv_general_dilated(
        full, w_dw[:, None, :].astype(u.dtype), window_strides=(1,), padding='VALID',
        dimension_numbers=('NWC', 'WIO', 'NWC'), feature_group_count=d) + b_dw
    c = jax.nn.silu(layer_norm(c, ln_g, ln_b))
    return c @ w_pw2 + b_pw2, new_buf


def stick_breaking_weights(z, mask):
    log_beta = jax.nn.log_sigmoid(z)
    log_1m = jnp.where(mask, jax.nn.log_sigmoid(-z), 0.0)
    ax = z.ndim - 1
    tail = lax.cumsum(log_1m, axis=ax, reverse=True) - log_1m
    return jnp.where(mask, jnp.exp(log_beta + tail), 0.0)


def stick_breaking_prompt(q, k, v, bias):
    b, t, h, dh = q.shape
    nb = t // Q_BLOCK
    scale = 1.0 / math.sqrt(dh)
    q_blocks = q.reshape(b, nb, Q_BLOCK, h, dh).transpose(1, 0, 2, 3, 4)
    kpos = jnp.arange(t)
    bias_f = bias.astype(jnp.float32)[None, :, None, None]

    def block(args):
        q_blk, start = args
        z = jnp.einsum('bqhd,bkhd->bhqk', q_blk, k,
                       preferred_element_type=jnp.float32) * scale + bias_f
        qpos = start + jnp.arange(Q_BLOCK)
        mask = kpos[None, :] < qpos[:, None]
        a = stick_breaking_weights(z, mask)
        return jnp.einsum('bhqk,bkhd->bqhd', a, v.astype(jnp.float32)).astype(q.dtype)

    out = lax.map(block, (q_blocks, jnp.arange(nb) * Q_BLOCK))
    return out.transpose(1, 0, 2, 3, 4).reshape(b, t, h, dh)


def stick_breaking_sample(q, k_new, v_new, k_past, v_past, bias):
    p = k_past.shape[1]
    tn = q.shape[1]
    scale = 1.0 / math.sqrt(q.shape[-1])
    z = jnp.concatenate([
        jnp.einsum('bqhd,bkhd->bhqk', q, k_past, preferred_element_type=jnp.float32),
        jnp.einsum('bqhd,bkhd->bhqk', q, k_new, preferred_element_type=jnp.float32)],
        axis=-1) * scale + bias.astype(jnp.float32)[None, :, None, None]
    kpos = jnp.arange(p + tn)
    qpos = p + jnp.arange(tn)
    mask = kpos[None, :] < qpos[:, None]
    a = stick_breaking_weights(z, mask)
    out = (jnp.einsum('bhqk,bkhd->bqhd', a[..., :p], v_past.astype(jnp.float32))
           + jnp.einsum('bhqk,bkhd->bqhd', a[..., p:], v_new.astype(jnp.float32)))
    return out.astype(q.dtype)


def setup_inputs(seed: int = 0) -> dict:
    key = jax.random.key(seed)
    ks = jax.random.split(key, 32)
    f32 = jnp.float32
    n_pages = PAST_LEN // PAGE_SIZE
    n_used = DEC_BATCH * n_pages
    n_phys = n_used + max(1, n_used // 4)
    hd = N_HEADS * HEAD_DIM

    def nrm(k, shape, scale=1.0):
        return jax.random.normal(k, shape, f32) * scale

    page_table = jax.random.permutation(ks[5], n_phys)[:n_used].reshape(DEC_BATCH, n_pages).astype(jnp.int32)
    return {
        'x_prompt': nrm(ks[0], (BATCH, SEQ, D_MODEL)),
        'x_sample': nrm(ks[1], (DEC_BATCH, DEC_SEQ, D_MODEL)),
        'state_conv': nrm(ks[2], (N_A_LAYERS, DEC_BATCH, CONV_WIDTH - 1, D_MODEL), 0.5),
        'cache_k': nrm(ks[3], (n_phys, PAGE_SIZE, N_HEADS, HEAD_DIM)),
        'cache_v': nrm(ks[4], (n_phys, PAGE_SIZE, N_HEADS, HEAD_DIM)),
        'page_table': page_table,
        'norm_mix': 1.0 + nrm(ks[6], (DEPTH, D_MODEL), 0.02),
        'norm_ffn': 1.0 + nrm(ks[7], (DEPTH, D_MODEL), 0.02),
        'conv_w_pw1': nrm(ks[8], (N_A_LAYERS, D_MODEL, 2 * D_MODEL), D_MODEL ** -0.5),
        'conv_b_pw1': nrm(ks[9], (N_A_LAYERS, 2 * D_MODEL), 0.02),
        'conv_w_dw': nrm(ks[10], (N_A_LAYERS, CONV_WIDTH, D_MODEL), CONV_WIDTH ** -0.5),
        'conv_b_dw': nrm(ks[11], (N_A_LAYERS, D_MODEL), 0.02),
        'conv_ln_g': 1.0 + nrm(ks[12], (N_A_LAYERS, D_MODEL), 0.02),
        'conv_ln_b': nrm(ks[13], (N_A_LAYERS, D_MODEL), 0.02),
        'conv_w_pw2': nrm(ks[14], (N_A_LAYERS, D_MODEL, D_MODEL), D_MODEL ** -0.5),
        'conv_b_pw2': nrm(ks[15], (N_A_LAYERS, D_MODEL), 0.02),
        'kv_norm': 1.0 + nrm(ks[16], (D_MODEL,), 0.02),
        'w_kv': nrm(ks[17], (D_MODEL, 2 * hd), D_MODEL ** -0.5),
        'w_q': nrm(ks[18], (N_B_LAYERS, D_MODEL, hd), 0.5 * D_MODEL ** -0.5),
        'w_o': nrm(ks[19], (N_B_LAYERS, hd, D_MODEL), hd ** -0.5),
        'b_logit': LOGIT_BIAS_INIT + nrm(ks[24], (N_B_LAYERS, N_HEADS), 0.5),
        'ffn_w_gate': nrm(ks[20], (DEPTH, D_MODEL, D_FF), D_MODEL ** -0.5),
        'ffn_w_up': nrm(ks[21], (DEPTH, D_MODEL, D_FF), D_MODEL ** -0.5),
        'ffn_w_down': nrm(ks[22], (DEPTH, D_FF, D_MODEL), D_FF ** -0.5),
        'final_norm': 1.0 + nrm(ks[23], (D_MODEL,), 0.02),
    }


def reference(x_prompt, x_sample, state_conv, cache_k, cache_v, page_table,
              norm_mix, norm_ffn, conv_w_pw1, conv_b_pw1, conv_w_dw, conv_b_dw,
              conv_ln_g, conv_ln_b, conv_w_pw2, conv_b_pw2, kv_norm, w_kv, w_q, w_o,
              b_logit, ffn_w_gate, ffn_w_up, ffn_w_down, final_norm):
    xp, xs = x_prompt, x_sample
    dec_b, n_pages = page_table.shape
    hd = N_HEADS * HEAD_DIM

    k_past = cache_k[page_table].reshape(dec_b, n_pages * PAGE_SIZE, N_HEADS, HEAD_DIM)
    v_past = cache_v[page_table].reshape(dec_b, n_pages * PAGE_SIZE, N_HEADS, HEAD_DIM)

    def shared_kv(x):
        kv = rms_norm(x, kv_norm) @ w_kv
        b, t = x.shape[0], x.shape[1]
        return (kv[..., :hd].reshape(b, t, N_HEADS, HEAD_DIM),
                kv[..., hd:].reshape(b, t, N_HEADS, HEAD_DIM))

    conv_bufs_p, conv_bufs_s = [], []
    k_p = v_p = k_s = v_s = None
    for l in range(DEPTH):
        if l < N_A_LAYERS:
            prm = (conv_w_pw1[l], conv_b_pw1[l], conv_w_dw[l], conv_b_dw[l],
                   conv_ln_g[l], conv_ln_b[l], conv_w_pw2[l], conv_b_pw2[l])
            zero_buf = jnp.zeros((xp.shape[0], CONV_WIDTH - 1, xp.shape[-1]), xp.dtype)
            hp, bp = conformer_conv(rms_norm(xp, norm_mix[l]), zero_buf, *prm)
            hs, bs = conformer_conv(rms_norm(xs, norm_mix[l]), state_conv[l], *prm)
            conv_bufs_p.append(bp)
            conv_bufs_s.append(bs)
        else:
            if l == N_A_LAYERS:
                k_p, v_p = shared_kv(xp)
                k_s, v_s = shared_kv(xs)
            j = l - N_A_LAYERS
            qp = (rms_norm(xp, norm_mix[l]) @ w_q[j]).reshape(xp.shape[0], xp.shape[1], N_HEADS, HEAD_DIM)
            qs = (rms_norm(xs, norm_mix[l]) @ w_q[j]).reshape(xs.shape[0], xs.shape[1], N_HEADS, HEAD_DIM)
            op = stick_breaking_prompt(qp, k_p, v_p, b_logit[j])
            os_ = stick_breaking_sample(qs, k_s, v_s, k_past, v_past, b_logit[j])
            hp = op.reshape(xp.shape[0], xp.shape[1], hd) @ w_o[j]
            hs = os_.reshape(xs.shape[0], xs.shape[1], hd) @ w_o[j]
        xp = xp + hp
        xs = xs + hs
        xp = xp + swiglu(rms_norm(xp, norm_ffn[l]), ffn_w_gate[l], ffn_w_up[l], ffn_w_down[l])
        xs = xs + swiglu(rms_norm(xs, norm_ffn[l]), ffn_w_gate[l], ffn_w_up[l], ffn_w_down[l])

    y_prompt = rms_norm(xp, final_norm)
    y_sample = rms_norm(xs, final_norm)
    state_conv_prompt = jnp.stack(conv_bufs_p, axis=0)
    state_conv_sample = jnp.stack(conv_bufs_s, axis=0)
    return (y_prompt, y_sample, state_conv_prompt, state_conv_sample, k_p, v_p, k_s, v_s)
```

```python
import functools
import math

import jax
import jax.numpy as jnp
from jax import lax
from jax.experimental import pallas as pl
from jax.experimental.pallas import tpu as pltpu

F32 = jnp.float32
BF16 = jnp.bfloat16

N_HEADS = 16
HEAD_DIM = 64
CONV_WIDTH = 31
PAGE_SIZE = 128
RMS_EPS = 1e-6
LN_EPS = 1e-5

LANES = 128
HEADS_PER_LANE_BLOCK = LANES // HEAD_DIM
CONV_HALO = 32
VMEM_LIMIT = 56 * 1024 * 1024

TOKEN_TILE = 512
CONV_TILE = 256
ATTN_Q_BLOCK = 128
ATTN_K_BLOCK = 256


def _params(*sem):
    return pltpu.CompilerParams(dimension_semantics=sem, vmem_limit_bytes=VMEM_LIMIT)


def _const_spec(shape):
    nd = len(shape)
    return pl.BlockSpec(shape, lambda *_: (0,) * nd, pipeline_mode=pl.Buffered(1))


def _rms(x, g):
    ms = jnp.mean(x * x, axis=-1, keepdims=True)
    return x * lax.rsqrt(ms + RMS_EPS) * g


def _silu(x):
    return x * jax.nn.sigmoid(x)


def _dot(a, b):
    return jnp.dot(a, b, preferred_element_type=F32)


def _dot_nt(a, b):
    return lax.dot_general(a, b, (((1,), (1,)), ((), ())), preferred_element_type=F32)


def _pw1_glu_kernel(x_ref, g_ref, w_ref, b_ref, u_ref):
    d = u_ref.shape[-1]
    h = _rms(x_ref[...], g_ref[...]).astype(BF16)
    val = _dot(h, w_ref[:, :d]) + b_ref[:, :d]
    gate = _dot(h, w_ref[:, d:]) + b_ref[:, d:]
    u_ref[...] = val * jax.nn.sigmoid(gate)


def _pw1_glu(x, g, w, b):
    n, d = x.shape
    tm = min(TOKEN_TILE, n)
    return pl.pallas_call(
        _pw1_glu_kernel,
        out_shape=jax.ShapeDtypeStruct((n, d), F32),
        grid=(n // tm,),
        in_specs=[pl.BlockSpec((tm, d), lambda i: (i, 0)),
                  _const_spec((1, d)), _const_spec(w.shape), _const_spec((1, 2 * d))],
        out_specs=pl.BlockSpec((tm, d), lambda i: (i, 0)),
        compiler_params=_params("parallel"),
        name="pw1_glu",
    )(x, g.reshape(1, d), w, b.reshape(1, 2 * d))


def _ffn_kernel(*refs, chunks, pre_proj, final_norm):
    refs = list(refs)
    x_ref = refs.pop(0)
    x = x_ref[...]
    if pre_proj:
        a_ref, wo_ref = refs.pop(0), refs.pop(0)
        x = x + _dot(a_ref[...], wo_ref[...])
    g_ref, wg_ref, wu_ref, wd_ref = refs[:4]
    refs = refs[4:]
    h = _rms(x, g_ref[...]).astype(BF16)
    y = x
    for c0, c1 in chunks:
        gate = _dot(h, wg_ref[:, c0:c1])
        up = _dot(h, wu_ref[:, c0:c1])
        act = (_silu(gate) * up).astype(BF16)
        y = y + _dot(act, wd_ref[c0:c1, :])
    if final_norm:
        fg_ref, o_ref = refs
        o_ref[...] = _rms(y, fg_ref[...])
    else:
        (o_ref,) = refs
        o_ref[...] = y


def _ffn(x, g, wg, wu, wd, attn=None, wo=None, final_gain=None):
    n, d = x.shape
    dff = wg.shape[1]
    tm = min(TOKEN_TILE, n)
    step = 2 * LANES * 3
    chunks = tuple((c, min(c + step, dff)) for c in range(0, dff, step))
    row = lambda i: (i, 0)
    args, specs = [x], [pl.BlockSpec((tm, d), row)]
    if attn is not None:
        args += [attn, wo]
        specs += [pl.BlockSpec((tm, attn.shape[1]), row), _const_spec(wo.shape)]
    args += [g.reshape(1, d), wg, wu, wd]
    specs += [_const_spec((1, d)), _const_spec(wg.shape), _const_spec(wu.shape), _const_spec(wd.shape)]
    if final_gain is not None:
        args.append(final_gain.reshape(1, d))
        specs.append(_const_spec((1, d)))
    return pl.pallas_call(
        functools.partial(_ffn_kernel, chunks=chunks, pre_proj=attn is not None,
                          final_norm=final_gain is not None),
        out_shape=jax.ShapeDtypeStruct((n, d), F32),
        grid=(n // tm,),
        in_specs=specs,
        out_specs=pl.BlockSpec((tm, d), row),
        compiler_params=_params("parallel"),
        name="ffn",
    )(*args)


def _kv_proj_kernel(x_ref, g_ref, w_ref, k_ref, v_ref, kb_ref, vb_ref):
    hd = k_ref.shape[-1]
    h = _rms(x_ref[...], g_ref[...]).astype(BF16)
    k = _dot(h, w_ref[:, :hd])
    v = _dot(h, w_ref[:, hd:])
    k_ref[...] = k
    v_ref[...] = v
    kb_ref[...] = k.astype(BF16)
    vb_ref[...] = v.astype(BF16)


def _kv_proj(x, g, w):
    n, d = x.shape
    hd = w.shape[1] // 2
    tm = min(TOKEN_TILE, n)
    row = lambda i: (i, 0)
    out = pl.BlockSpec((tm, hd), row)
    return pl.pallas_call(
        _kv_proj_kernel,
        out_shape=(jax.ShapeDtypeStruct((n, hd), F32), jax.ShapeDtypeStruct((n, hd), F32),
                   jax.ShapeDtypeStruct((n, hd), BF16), jax.ShapeDtypeStruct((n, hd), BF16)),
        grid=(n // tm,),
        in_specs=[pl.BlockSpec((tm, d), row), _const_spec((1, d)), _const_spec(w.shape)],
        out_specs=(out, out, out, out),
        compiler_params=_params("parallel"),
        name="kv_proj",
    )(x, g.reshape(1, d), w)


def _q_proj_kernel(x_ref, g_ref, w_ref, q_ref, *, scale):
    h = _rms(x_ref[...], g_ref[...]).astype(BF16)
    q_ref[...] = (_dot(h, w_ref[...]) * scale).astype(BF16)


def _q_proj(x, g, w):
    n, d = x.shape
    hd = w.shape[1]
    tm = min(TOKEN_TILE, n)
    row = lambda i: (i, 0)
    return pl.pallas_call(
        functools.partial(_q_proj_kernel, scale=1.0 / math.sqrt(HEAD_DIM)),
        out_shape=jax.ShapeDtypeStruct((n, hd), BF16),
        grid=(n // tm,),
        in_specs=[pl.BlockSpec((tm, d), row), _const_spec((1, d)), _const_spec(w.shape)],
        out_specs=pl.BlockSpec((tm, hd), row),
        compiler_params=_params("parallel"),
        name="q_proj",
    )(x, g.reshape(1, d), w)


def _ln_swish_pw2(c, x, lng_ref, lnb_ref, w2_ref, b2_ref):
    mu = jnp.mean(c, axis=-1, keepdims=True)
    xc = c - mu
    var = jnp.mean(xc * xc, axis=-1, keepdims=True)
    y = _silu(xc * lax.rsqrt(var + LN_EPS) * lng_ref[...] + lnb_ref[...])
    return x + _dot(y.astype(BF16), w2_ref[...]) + b2_ref[...]


def _conv_prompt_kernel(u_ref, uh_ref, x_ref, wdw_ref, bdw_ref, lng_ref, lnb_ref, w2_ref, b2_ref,
                        o_ref, win_ref, c_ref):
    i = pl.program_id(1)
    tm, d = c_ref.shape
    win_ref[0:CONV_HALO, :] = jnp.where(i == 0, 0.0, uh_ref[0])
    win_ref[CONV_HALO:, :] = u_ref[0]
    first = CONV_HALO - (CONV_WIDTH - 1)
    rows, cols = 32, 256
    for r in range(0, tm, rows):
        for c in range(0, d, cols):
            acc = jnp.broadcast_to(bdw_ref[:, c:c + cols], (rows, cols))
            for w in range(CONV_WIDTH):
                acc = acc + win_ref[r + first + w:r + first + w + rows, c:c + cols] * wdw_ref[w:w + 1, c:c + cols]
            c_ref[r:r + rows, c:c + cols] = acc
    o_ref[0] = _ln_swish_pw2(c_ref[...], x_ref[0], lng_ref, lnb_ref, w2_ref, b2_ref)


def _conv_prompt(u, x, wdw, bdw, lng, lnb, w2, b2):
    b, t, d = u.shape
    tm = CONV_TILE
    per = tm // CONV_HALO
    tile = pl.BlockSpec((1, tm, d), lambda bi, i: (bi, i, 0))
    halo = pl.BlockSpec((1, CONV_HALO, d), lambda bi, i: (bi, jnp.maximum(i * per - 1, 0), 0))
    vec = _const_spec((1, d))
    return pl.pallas_call(
        _conv_prompt_kernel,
        out_shape=jax.ShapeDtypeStruct((b, t, d), F32),
        grid=(b, t // tm),
        in_specs=[tile, halo, tile, _const_spec(wdw.shape), vec, vec, vec, _const_spec(w2.shape), vec],
        out_specs=tile,
        scratch_shapes=[pltpu.VMEM((CONV_HALO + tm, d), F32), pltpu.VMEM((tm, d), F32)],
        compiler_params=_params("parallel", "parallel"),
        name="conv_prompt",
    )(u, u, x, wdw, bdw.reshape(1, d), lng.reshape(1, d), lnb.reshape(1, d), w2, b2.reshape(1, d))


def _conv_sample_kernel(full_ref, x_ref, wdw_ref, bdw_ref, lng_ref, lnb_ref, w2_ref, b2_ref, o_ref):
    tn, bs, d = x_ref.shape
    for t in range(tn):
        acc = jnp.broadcast_to(bdw_ref[...], (bs, d))
        for w in range(CONV_WIDTH):
            acc = acc + full_ref[t + w] * wdw_ref[w:w + 1, :]
        o_ref[t] = _ln_swish_pw2(acc, x_ref[t], lng_ref, lnb_ref, w2_ref, b2_ref)


def _conv_sample(full, x, wdw, bdw, lng, lnb, w2, b2):
    tn, nb, d = x.shape
    bs = 32
    vec = _const_spec((1, d))
    return pl.pallas_call(
        _conv_sample_kernel,
        out_shape=jax.ShapeDtypeStruct((tn, nb, d), F32),
        grid=(nb // bs,),
        in_specs=[pl.BlockSpec((full.shape[0], bs, d), lambda i: (0, i, 0)),
                  pl.BlockSpec((tn, bs, d), lambda i: (0, i, 0)),
                  _const_spec(wdw.shape), vec, vec, vec, _const_spec(w2.shape), vec],
        out_specs=pl.BlockSpec((tn, bs, d), lambda i: (0, i, 0)),
        compiler_params=_params("parallel"),
        name="conv_sample",
    )(full, x, wdw, bdw.reshape(1, d), lng.reshape(1, d), lnb.reshape(1, d), w2, b2.reshape(1, d))


def _suffix_ones(n):
    j = lax.broadcasted_iota(jnp.int32, (n, n), 0)
    s = lax.broadcasted_iota(jnp.int32, (n, n), 1)
    return (j >= s).astype(BF16)


def _stick_tile(z, carry, u_incl, mask):
    sp = jnp.maximum(z, 0.0) + jnp.log(1.0 + jnp.exp(-jnp.abs(z)))
    if mask is not None:
        sp = jnp.where(mask, sp, 0.0)
    hi = sp.astype(BF16)
    lo = (sp - hi.astype(F32)).astype(BF16)
    s_incl = _dot(hi, u_incl) + _dot(lo, u_incl)
    a = jnp.exp(z - s_incl - carry)
    if mask is not None:
        a = jnp.where(mask, a, 0.0)
    return a, carry + jnp.sum(sp, axis=-1, keepdims=True)


def _attn_prompt_kernel(bias_ref, q_ref, k_ref, v_ref, u_ref, o_ref, *, qb, kb):
    p = pl.program_id(1)
    i = pl.program_id(2)
    q2 = q_ref[0]
    lane = lax.broadcasted_iota(jnp.int32, (qb, LANES), 1)
    q_pos = i * qb + lax.broadcasted_iota(jnp.int32, (qb, kb), 0)
    k_off = lax.broadcasted_iota(jnp.int32, (qb, kb), 1)
    u_incl = u_ref[...]
    diag = (i * qb) // kb
    outs = []
    for hh in range(HEADS_PER_LANE_BLOCK):
        bias = bias_ref[HEADS_PER_LANE_BLOCK * p + hh]
        in_head = (lane >= hh * HEAD_DIM) & (lane < (hh + 1) * HEAD_DIM)
        qm = jnp.where(in_head, q2, jnp.zeros_like(q2))

        def tile(j, carry, acc, masked):
            start = pl.multiple_of(j * kb, kb)
            k_blk = k_ref[0, pl.ds(start, kb), :]
            v_blk = v_ref[0, pl.ds(start, kb), :]
            z = _dot_nt(qm, k_blk) + bias
            mask = (start + k_off < q_pos) if masked else None
            a, carry = _stick_tile(z, carry, u_incl, mask)
            return carry, acc + _dot(a.astype(BF16), v_blk)

        state = tile(diag, jnp.zeros((qb, 1), F32), jnp.zeros((qb, LANES), F32), True)
        _, acc = lax.fori_loop(
            0, diag, lambda s, st: tile(diag - 1 - s, st[0], st[1], False), state)
        outs.append((in_head, acc))
    out = jnp.zeros((qb, LANES), F32)
    for in_head, acc in outs:
        out = jnp.where(in_head, acc, out)
    o_ref[0] = out.astype(o_ref.dtype)


def _attn_prompt(q, k, v, bias):
    b, t, hd = q.shape
    qb, kb = ATTN_Q_BLOCK, ATTN_K_BLOCK
    kv_spec = pl.BlockSpec((1, t, LANES), lambda bi, p, i: (bi, 0, p))
    q_spec = pl.BlockSpec((1, qb, LANES), lambda bi, p, i: (bi, i, p))
    return pl.pallas_call(
        functools.partial(_attn_prompt_kernel, qb=qb, kb=kb),
        out_shape=jax.ShapeDtypeStruct((b, t, hd), BF16),
        grid=(b, hd // LANES, t // qb),
        in_specs=[pl.BlockSpec(memory_space=pltpu.SMEM), q_spec, kv_spec, kv_spec,
                  _const_spec((kb, kb))],
        out_specs=q_spec,
        compiler_params=_params("parallel", "parallel", "arbitrary"),
        name="attn_prompt",
    )(bias, q, k, v, _suffix_ones(kb))


def _attn_sample_kernel(pt_ref, q_ref, bias_ref, kn_ref, vn_ref, kp_ref, vp_ref, u_ref, o_ref,
                        qx_ref, carry_ref, acc_ref):
    del pt_ref
    s = pl.program_id(1)
    tn = q_ref.shape[1]
    nr = tn * N_HEADS
    hd = N_HEADS * HEAD_DIM
    row = lax.broadcasted_iota(jnp.int32, (nr, hd), 0)
    lane_head = lax.broadcasted_iota(jnp.int32, (nr, hd), 1) // HEAD_DIM
    own_head = (row % N_HEADS) == lane_head

    def visit(k_bf, v_bf, mask):
        z = _dot_nt(qx_ref[...], k_bf) + bias_ref[...]
        a, carry = _stick_tile(z, carry_ref[...], u_ref[...], mask)
        carry_ref[...] = carry
        acc_ref[...] += _dot(a.astype(BF16), v_bf)

    @pl.when(s == 0)
    def _():
        q4 = q_ref[0].astype(F32)
        rows = jnp.concatenate(
            [jnp.broadcast_to(q4[i:i + 1], (N_HEADS, hd)) for i in range(tn)], axis=0)
        qx_ref[...] = jnp.where(own_head, rows, 0.0).astype(BF16)
        carry_ref[...] = jnp.zeros_like(carry_ref)
        acc_ref[...] = jnp.zeros_like(acc_ref)
        pad = jnp.zeros((PAGE_SIZE - kn_ref.shape[1], hd), F32)
        k_new = jnp.concatenate([kn_ref[0], pad], axis=0).astype(BF16)
        v_new = jnp.concatenate([vn_ref[0], pad], axis=0).astype(BF16)
        q_idx = lax.broadcasted_iota(jnp.int32, (nr, PAGE_SIZE), 0) // N_HEADS
        k_idx = lax.broadcasted_iota(jnp.int32, (nr, PAGE_SIZE), 1)
        visit(k_new, v_new, k_idx < q_idx)

    @pl.when(s > 0)
    def _():
        visit(kp_ref[0].astype(BF16), vp_ref[0].astype(BF16), None)

    @pl.when(s == pl.num_programs(1) - 1)
    def _():
        sel = jnp.where(own_head, acc_ref[...], 0.0).reshape(tn, N_HEADS, hd)
        o_ref[0] = jnp.sum(sel, axis=1).astype(o_ref.dtype)


def _attn_sample(q, k_new, v_new, cache_k, cache_v, page_table, bias):
    bd, tn, hd = q.shape
    n_pages = page_table.shape[1]
    nr = tn * N_HEADS
    bias_rows = jnp.tile(bias, tn).reshape(nr, 1)

    def page(b, s, pt):
        return (pt[b, n_pages - 1 - jnp.maximum(s - 1, 0)], 0, 0)

    per_b = lambda b, s, pt: (b, 0, 0)
    const2 = lambda b, s, pt: (0, 0)
    grid_spec = pltpu.PrefetchScalarGridSpec(
        num_scalar_prefetch=1,
        grid=(bd, n_pages + 1),
        in_specs=[pl.BlockSpec((1, tn, hd), per_b),
                  pl.BlockSpec((nr, 1), const2),
                  pl.BlockSpec((1, k_new.shape[1], hd), per_b),
                  pl.BlockSpec((1, v_new.shape[1], hd), per_b),
                  pl.BlockSpec((1, PAGE_SIZE, hd), page),
                  pl.BlockSpec((1, PAGE_SIZE, hd), page),
                  pl.BlockSpec((PAGE_SIZE, PAGE_SIZE), const2)],
        out_specs=pl.BlockSpec((1, tn, hd), per_b),
        scratch_shapes=[pltpu.VMEM((nr, hd), BF16), pltpu.VMEM((nr, 1), F32),
                        pltpu.VMEM((nr, hd), F32)],
    )
    return pl.pallas_call(
        _attn_sample_kernel,
        out_shape=jax.ShapeDtypeStruct((bd, tn, hd), BF16),
        grid_spec=grid_spec,
        compiler_params=_params("parallel", "arbitrary"),
        name="attn_sample",
    )(page_table, q, bias_rows, k_new, v_new, cache_k, cache_v, _suffix_ones(PAGE_SIZE))


def kernel(x_prompt, x_sample, state_conv, cache_k, cache_v, page_table, norm_mix, norm_ffn, conv_w_pw1, conv_b_pw1, conv_w_dw, conv_b_dw, conv_ln_g, conv_ln_b, conv_w_pw2, conv_b_pw2, kv_norm, w_kv, w_q, w_o, b_logit, ffn_w_gate, ffn_w_up, ffn_w_down, final_norm):
    b, t, d = x_prompt.shape
    bd, tn, _ = x_sample.shape
    hd = N_HEADS * HEAD_DIM
    depth = norm_mix.shape[0]
    n_conv = conv_w_pw1.shape[0]
    n_phys = cache_k.shape[0]
    keep = CONV_WIDTH - 1

    to_time_major = lambda a: a.transpose(1, 0, 2)
    xp = x_prompt.reshape(b * t, d)
    xs = to_time_major(x_sample).reshape(tn * bd, d)
    cache_k = cache_k.reshape(n_phys, PAGE_SIZE, hd)
    cache_v = cache_v.reshape(n_phys, PAGE_SIZE, hd)

    conv_state_p, conv_state_s = [], []
    kv_p = kv_s = None
    y_p = y_s = None
    for l in range(depth):
        last = final_norm if l == depth - 1 else None
        ffn_w = (norm_ffn[l], ffn_w_gate[l].astype(BF16), ffn_w_up[l].astype(BF16),
                 ffn_w_down[l].astype(BF16))
        if l < n_conv:
            w1 = conv_w_pw1[l].astype(BF16)
            tail = (conv_w_dw[l], conv_b_dw[l], conv_ln_g[l], conv_ln_b[l],
                    conv_w_pw2[l].astype(BF16), conv_b_pw2[l])
            u_p = _pw1_glu(xp, norm_mix[l], w1, conv_b_pw1[l]).reshape(b, t, d)
            u_s = _pw1_glu(xs, norm_mix[l], w1, conv_b_pw1[l]).reshape(tn, bd, d)
            conv_state_p.append(u_p[:, t - keep:])
            full_s = jnp.concatenate([to_time_major(state_conv[l]), u_s], axis=0)
            conv_state_s.append(to_time_major(full_s[tn:]))
            xp = _conv_prompt(u_p, xp.reshape(b, t, d), *tail).reshape(b * t, d)
            xs = _conv_sample(full_s, xs.reshape(tn, bd, d), *tail).reshape(tn * bd, d)
            xp = _ffn(xp, *ffn_w, final_gain=last)
            xs = _ffn(xs, *ffn_w, final_gain=last)
        else:
            j = l - n_conv
            if kv_p is None:
                wkv = w_kv.astype(BF16)
                kv_p = _kv_proj(xp, kv_norm, wkv)
                kv_s = _kv_proj(xs, kv_norm, wkv)
                new_rows = [jnp.pad(to_time_major(a.reshape(tn, bd, hd)), ((0, 0), (0, 8 - tn), (0, 0)))
                            for a in kv_s[:2]]
            wq = w_q[j].astype(BF16)
            wo = w_o[j].astype(BF16)
            q_p = _q_proj(xp, norm_mix[l], wq).reshape(b, t, hd)
            q_s = to_time_major(_q_proj(xs, norm_mix[l], wq).reshape(tn, bd, hd))
            a_p = _attn_prompt(q_p, kv_p[2].reshape(b, t, hd), kv_p[3].reshape(b, t, hd), b_logit[j])
            a_s = _attn_sample(q_s, new_rows[0], new_rows[1], cache_k, cache_v, page_table, b_logit[j])
            a_s = to_time_major(a_s).reshape(tn * bd, hd)
            xp = _ffn(xp, *ffn_w, attn=a_p.reshape(b * t, hd), wo=wo, final_gain=last)
            xs = _ffn(xs, *ffn_w, attn=a_s, wo=wo, final_gain=last)

    y_p = xp.reshape(b, t, d)
    y_s = to_time_major(xs.reshape(tn, bd, d))
    heads = lambda a, n0, n1: a.reshape(n0, n1, N_HEADS, HEAD_DIM)
    k_s = to_time_major(kv_s[0].reshape(tn, bd, hd))
    v_s = to_time_major(kv_s[1].reshape(tn, bd, hd))
    return (y_p, y_s, jnp.stack(conv_state_p, axis=0), jnp.stack(conv_state_s, axis=0),
            heads(kv_p[0], b, t), heads(kv_p[1], b, t), heads(k_s, bd, tn), heads(v_s, bd, tn))
```

```python
import functools
import math

import jax
import jax.numpy as jnp
from jax import lax
from jax.experimental import pallas as pl
from jax.experimental.pallas import tpu as pltpu

F32 = jnp.float32
BF16 = jnp.bfloat16

N_HEADS = 16
HEAD_DIM = 64
CONV_WIDTH = 31
PAGE_SIZE = 128
RMS_EPS = 1e-6
LN_EPS = 1e-5
LOG2_E = math.log2(math.e)

LANES = 128
HEADS_PER_LANE_BLOCK = LANES // HEAD_DIM
CONV_HALO = 32
VMEM_LIMIT = 56 * 1024 * 1024

TOKEN_TILE = 512
CONV_TILE = 256
ATTN_Q_BLOCK = 512
ATTN_K_BLOCK = 256
SAMPLE_PAGE_GROUP = 4


def _params(*sem):
    return pltpu.CompilerParams(dimension_semantics=sem, vmem_limit_bytes=VMEM_LIMIT)


def _const_spec(shape):
    nd = len(shape)
    return pl.BlockSpec(shape, lambda *_: (0,) * nd, pipeline_mode=pl.Buffered(1))


def _rms(x, g):
    ms = jnp.mean(x * x, axis=-1, keepdims=True)
    return x * lax.rsqrt(ms + RMS_EPS) * g


def _silu(x):
    return x * jax.nn.sigmoid(x)


def _dot(a, b):
    return jnp.dot(a, b, preferred_element_type=F32)


def _dot_nt(a, b):
    return lax.dot_general(a, b, (((1,), (1,)), ((), ())), preferred_element_type=F32)


def _pw1_glu_kernel(x_ref, g_ref, w_ref, b_ref, u_ref):
    d = u_ref.shape[-1]
    h = _rms(x_ref[...], g_ref[...]).astype(BF16)
    val = _dot(h, w_ref[:, :d]) + b_ref[:, :d]
    gate = _dot(h, w_ref[:, d:]) + b_ref[:, d:]
    u_ref[...] = val * jax.nn.sigmoid(gate)


def _pw1_glu(x, g, w, b):
    n, d = x.shape
    tm = min(TOKEN_TILE, n)
    return pl.pallas_call(
        _pw1_glu_kernel,
        out_shape=jax.ShapeDtypeStruct((n, d), F32),
        grid=(n // tm,),
        in_specs=[pl.BlockSpec((tm, d), lambda i: (i, 0)),
                  _const_spec((1, d)), _const_spec(w.shape), _const_spec((1, 2 * d))],
        out_specs=pl.BlockSpec((tm, d), lambda i: (i, 0)),
        compiler_params=_params("parallel"),
        name="pw1_glu",
    )(x, g.reshape(1, d), w, b.reshape(1, 2 * d))


def _ffn_kernel(*refs, chunks, pre_proj, final_norm):
    refs = list(refs)
    x_ref = refs.pop(0)
    x = x_ref[...]
    if pre_proj:
        a_ref, wo_ref = refs.pop(0), refs.pop(0)
        x = x + _dot(a_ref[...], wo_ref[...])
    g_ref, wg_ref, wu_ref, wd_ref = refs[:4]
    refs = refs[4:]
    h = _rms(x, g_ref[...]).astype(BF16)
    y = x
    for c0, c1 in chunks:
        gate = _dot(h, wg_ref[:, c0:c1])
        up = _dot(h, wu_ref[:, c0:c1])
        act = (_silu(gate) * up).astype(BF16)
        y = y + _dot(act, wd_ref[c0:c1, :])
    if final_norm:
        fg_ref, o_ref = refs
        o_ref[...] = _rms(y, fg_ref[...])
    else:
        (o_ref,) = refs
        o_ref[...] = y


def _ffn(x, g, wg, wu, wd, attn=None, wo=None, final_gain=None):
    n, d = x.shape
    dff = wg.shape[1]
    tm = min(TOKEN_TILE, n)
    step = 2 * LANES * 3
    chunks = tuple((c, min(c + step, dff)) for c in range(0, dff, step))
    row = lambda i: (i, 0)
    args, specs = [x], [pl.BlockSpec((tm, d), row)]
    if attn is not None:
        args += [attn, wo]
        specs += [pl.BlockSpec((tm, attn.shape[1]), row), _const_spec(wo.shape)]
    args += [g.reshape(1, d), wg, wu, wd]
    specs += [_const_spec((1, d)), _const_spec(wg.shape), _const_spec(wu.shape), _const_spec(wd.shape)]
    if final_gain is not None:
        args.append(final_gain.reshape(1, d))
        specs.append(_const_spec((1, d)))
    return pl.pallas_call(
        functools.partial(_ffn_kernel, chunks=chunks, pre_proj=attn is not None,
                          final_norm=final_gain is not None),
        out_shape=jax.ShapeDtypeStruct((n, d), F32),
        grid=(n // tm,),
        in_specs=specs,
        out_specs=pl.BlockSpec((tm, d), row),
        compiler_params=_params("parallel"),
        name="ffn",
    )(*args)


def _kv_proj_kernel(x_ref, g_ref, w_ref, k_ref, v_ref, kb_ref, vb_ref):
    hd = k_ref.shape[-1]
    h = _rms(x_ref[...], g_ref[...]).astype(BF16)
    k = _dot(h, w_ref[:, :hd])
    v = _dot(h, w_ref[:, hd:])
    k_ref[...] = k
    v_ref[...] = v
    kb_ref[...] = k.astype(BF16)
    vb_ref[...] = v.astype(BF16)


def _kv_proj(x, g, w):
    n, d = x.shape
    hd = w.shape[1] // 2
    tm = min(TOKEN_TILE, n)
    row = lambda i: (i, 0)
    out = pl.BlockSpec((tm, hd), row)
    return pl.pallas_call(
        _kv_proj_kernel,
        out_shape=(jax.ShapeDtypeStruct((n, hd), F32), jax.ShapeDtypeStruct((n, hd), F32),
                   jax.ShapeDtypeStruct((n, hd), BF16), jax.ShapeDtypeStruct((n, hd), BF16)),
        grid=(n // tm,),
        in_specs=[pl.BlockSpec((tm, d), row), _const_spec((1, d)), _const_spec(w.shape)],
        out_specs=(out, out, out, out),
        compiler_params=_params("parallel"),
        name="kv_proj",
    )(x, g.reshape(1, d), w)


def _q_proj_kernel(x_ref, g_ref, w_ref, q_ref, *, scale):
    h = _rms(x_ref[...], g_ref[...]).astype(BF16)
    q_ref[...] = (_dot(h, w_ref[...]) * scale).astype(BF16)


def _q_proj(x, g, w):
    n, d = x.shape
    hd = w.shape[1]
    tm = min(TOKEN_TILE, n)
    row = lambda i: (i, 0)
    return pl.pallas_call(
        functools.partial(_q_proj_kernel, scale=LOG2_E / math.sqrt(HEAD_DIM)),
        out_shape=jax.ShapeDtypeStruct((n, hd), BF16),
        grid=(n // tm,),
        in_specs=[pl.BlockSpec((tm, d), row), _const_spec((1, d)), _const_spec(w.shape)],
        out_specs=pl.BlockSpec((tm, hd), row),
        compiler_params=_params("parallel"),
        name="q_proj",
    )(x, g.reshape(1, d), w)


def _ln_swish_pw2(c, x, lng_ref, lnb_ref, w2_ref, b2_ref):
    mu = jnp.mean(c, axis=-1, keepdims=True)
    xc = c - mu
    var = jnp.mean(xc * xc, axis=-1, keepdims=True)
    y = _silu(xc * lax.rsqrt(var + LN_EPS) * lng_ref[...] + lnb_ref[...])
    return x + _dot(y.astype(BF16), w2_ref[...]) + b2_ref[...]


def _conv_prompt_kernel(u_ref, uh_ref, x_ref, wdw_ref, bdw_ref, lng_ref, lnb_ref, w2_ref, b2_ref,
                        o_ref, win_ref, c_ref):
    i = pl.program_id(1)
    tm, d = c_ref.shape
    win_ref[0:CONV_HALO, :] = jnp.where(i == 0, 0.0, uh_ref[0])
    win_ref[CONV_HALO:, :] = u_ref[0]
    first = CONV_HALO - (CONV_WIDTH - 1)
    rows, cols = 32, 256
    for r in range(0, tm, rows):
        for c in range(0, d, cols):
            acc = jnp.broadcast_to(bdw_ref[:, c:c + cols], (rows, cols))
            for w in range(CONV_WIDTH):
                acc = acc + win_ref[r + first + w:r + first + w + rows, c:c + cols] * wdw_ref[w:w + 1, c:c + cols]
            c_ref[r:r + rows, c:c + cols] = acc
    o_ref[0] = _ln_swish_pw2(c_ref[...], x_ref[0], lng_ref, lnb_ref, w2_ref, b2_ref)


def _conv_prompt(u, x, wdw, bdw, lng, lnb, w2, b2):
    b, t, d = u.shape
    tm = CONV_TILE
    per = tm // CONV_HALO
    tile = pl.BlockSpec((1, tm, d), lambda bi, i: (bi, i, 0))
    halo = pl.BlockSpec((1, CONV_HALO, d), lambda bi, i: (bi, jnp.maximum(i * per - 1, 0), 0))
    vec = _const_spec((1, d))
    return pl.pallas_call(
        _conv_prompt_kernel,
        out_shape=jax.ShapeDtypeStruct((b, t, d), F32),
        grid=(b, t // tm),
        in_specs=[tile, halo, tile, _const_spec(wdw.shape), vec, vec, vec, _const_spec(w2.shape), vec],
        out_specs=tile,
        scratch_shapes=[pltpu.VMEM((CONV_HALO + tm, d), F32), pltpu.VMEM((tm, d), F32)],
        compiler_params=_params("parallel", "parallel"),
        name="conv_prompt",
    )(u, u, x, wdw, bdw.reshape(1, d), lng.reshape(1, d), lnb.reshape(1, d), w2, b2.reshape(1, d))


def _conv_sample_kernel(full_ref, x_ref, wdw_ref, bdw_ref, lng_ref, lnb_ref, w2_ref, b2_ref, o_ref):
    tn, bs, d = x_ref.shape
    for t in range(tn):
        acc = jnp.broadcast_to(bdw_ref[...], (bs, d))
        for w in range(CONV_WIDTH):
            acc = acc + full_ref[t + w] * wdw_ref[w:w + 1, :]
        o_ref[t] = _ln_swish_pw2(acc, x_ref[t], lng_ref, lnb_ref, w2_ref, b2_ref)


def _conv_sample(full, x, wdw, bdw, lng, lnb, w2, b2):
    tn, nb, d = x.shape
    bs = 32
    vec = _const_spec((1, d))
    return pl.pallas_call(
        _conv_sample_kernel,
        out_shape=jax.ShapeDtypeStruct((tn, nb, d), F32),
        grid=(nb // bs,),
        in_specs=[pl.BlockSpec((full.shape[0], bs, d), lambda i: (0, i, 0)),
                  pl.BlockSpec((tn, bs, d), lambda i: (0, i, 0)),
                  _const_spec(wdw.shape), vec, vec, vec, _const_spec(w2.shape), vec],
        out_specs=pl.BlockSpec((tn, bs, d), lambda i: (0, i, 0)),
        compiler_params=_params("parallel"),
        name="conv_sample",
    )(full, x, wdw, bdw.reshape(1, d), lng.reshape(1, d), lnb.reshape(1, d), w2, b2.reshape(1, d))


def _suffix_ones(n):
    j = lax.broadcasted_iota(jnp.int32, (n, n), 0)
    s = lax.broadcasted_iota(jnp.int32, (n, n), 1)
    return (j >= s).astype(BF16)


def _stick_tile(z, carry, u_incl, mask):
    neg_abs = lax.bitcast_convert_type(
        lax.bitcast_convert_type(z, jnp.uint32) | jnp.uint32(0x80000000), F32)
    sp = jnp.maximum(z, 0.0) + jnp.log2(1.0 + jnp.exp2(neg_abs))
    if mask is not None:
        sp = jnp.where(mask, sp, 0.0)
    s_incl = _dot(sp.astype(BF16), u_incl)
    reps = z.shape[1] // LANES
    a = jnp.exp2(z - s_incl - jnp.concatenate([carry] * reps, axis=1))
    if mask is not None:
        a = jnp.where(mask, a, 0.0)
    total = jnp.sum(sp, axis=-1, keepdims=True)
    return a.astype(BF16), carry + jnp.broadcast_to(total, carry.shape)


def _attn_prompt_kernel(bias_ref, q_ref, k_ref, v_ref, u_ref, o_ref, qs_ref, carry_ref, acc_ref,
                        *, qb, kb):
    p = pl.program_id(1)
    i = pl.program_id(2)
    nh = HEADS_PER_LANE_BLOCK
    q2 = q_ref[0]
    lane = lax.broadcasted_iota(jnp.int32, (qb, LANES), 1)
    for hh in range(nh):
        in_head = (lane >= hh * HEAD_DIM) & (lane < (hh + 1) * HEAD_DIM)
        qs_ref[hh * qb:(hh + 1) * qb, :] = jnp.where(in_head, q2, jnp.zeros_like(q2))
    carry_ref[...] = jnp.zeros_like(carry_ref)
    acc_ref[...] = jnp.zeros_like(acc_ref)
    bias = [bias_ref[nh * p + hh] for hh in range(nh)]

    def tile(j, masked):
        start = pl.multiple_of(j * kb, kb)
        k_blk = k_ref[0, pl.ds(start, kb), :]
        v_blk = v_ref[0, pl.ds(start, kb), :]
        s = _dot_nt(qs_ref[...], k_blk)
        z = jnp.concatenate([s[hh * qb:(hh + 1) * qb] + bias[hh] for hh in range(nh)], axis=0)
        mask = None
        if masked:
            row = lax.broadcasted_iota(jnp.int32, (nh * qb, kb), 0)
            q_pos = i * qb + row % qb
            mask = start + lax.broadcasted_iota(jnp.int32, (nh * qb, kb), 1) < q_pos
        a, carry = _stick_tile(z, carry_ref[...], u_ref[...], mask)
        carry_ref[...] = carry
        acc_ref[...] += _dot(a, v_blk)

    n_diag = qb // kb
    first = i * n_diag
    for s in range(n_diag):
        tile(first + n_diag - 1 - s, True)

    @pl.loop(0, i)
    def _(g):
        for s in range(n_diag):
            tile(first - 1 - g * n_diag - s, False)

    out = jnp.zeros((qb, LANES), F32)
    for hh in range(nh):
        in_head = (lane >= hh * HEAD_DIM) & (lane < (hh + 1) * HEAD_DIM)
        out = jnp.where(in_head, acc_ref[hh * qb:(hh + 1) * qb, :], out)
    o_ref[0] = out.astype(o_ref.dtype)


def _attn_prompt(q, k, v, bias):
    b, t, hd = q.shape
    qb, kb = ATTN_Q_BLOCK, ATTN_K_BLOCK
    rows = HEADS_PER_LANE_BLOCK * qb
    kv_spec = pl.BlockSpec((1, t, LANES), lambda bi, p, i: (bi, 0, p))
    q_spec = pl.BlockSpec((1, qb, LANES), lambda bi, p, i: (bi, i, p))
    return pl.pallas_call(
        functools.partial(_attn_prompt_kernel, qb=qb, kb=kb),
        out_shape=jax.ShapeDtypeStruct((b, t, hd), BF16),
        grid=(b, hd // LANES, t // qb),
        in_specs=[pl.BlockSpec(memory_space=pltpu.SMEM), q_spec, kv_spec, kv_spec,
                  _const_spec((kb, kb))],
        out_specs=q_spec,
        scratch_shapes=[pltpu.VMEM((rows, LANES), BF16), pltpu.VMEM((rows, LANES), F32),
                        pltpu.VMEM((rows, LANES), F32)],
        compiler_params=_params("parallel", "parallel", "arbitrary"),
        name="attn_prompt",
    )(bias, q, k, v, _suffix_ones(kb))


def _attn_sample_kernel(pt_ref, q_ref, bias_ref, kn_ref, vn_ref, *refs, group):
    del pt_ref
    kp_refs, vp_refs = refs[:group], refs[group:2 * group]
    uu_new_ref, uu_ref, o_ref, qx_ref, carry_ref, acc_ref = refs[2 * group:]
    s = pl.program_id(1)
    tn = q_ref.shape[1]
    nr = tn * N_HEADS
    hd = N_HEADS * HEAD_DIM
    row = lax.broadcasted_iota(jnp.int32, (nr, hd), 0)
    lane_head = lax.broadcasted_iota(jnp.int32, (nr, hd), 1) // HEAD_DIM
    own_head = (row % N_HEADS) == lane_head

    def visit(k_pages, v_pages, uu, mask):
        z = jnp.concatenate([_dot_nt(qx_ref[...], kp) for kp in k_pages], axis=1) + bias_ref[...]
        a, carry = _stick_tile(z, carry_ref[...], uu, mask)
        carry_ref[...] = carry
        acc = acc_ref[...]
        for r, vp in enumerate(v_pages):
            acc = acc + _dot(a[:, r * PAGE_SIZE:(r + 1) * PAGE_SIZE], vp)
        acc_ref[...] = acc

    @pl.when(s == 0)
    def _():
        q4 = q_ref[0].astype(F32)
        rows = jnp.concatenate(
            [jnp.broadcast_to(q4[i:i + 1], (N_HEADS, hd)) for i in range(tn)], axis=0)
        qx_ref[...] = jnp.where(own_head, rows, 0.0).astype(BF16)
        carry_ref[...] = jnp.zeros_like(carry_ref)
        acc_ref[...] = jnp.zeros_like(acc_ref)
        pad = jnp.zeros((PAGE_SIZE - kn_ref.shape[1], hd), F32)
        k_new = jnp.concatenate([kn_ref[0], pad], axis=0).astype(BF16)
        v_new = jnp.concatenate([vn_ref[0], pad], axis=0).astype(BF16)
        q_idx = lax.broadcasted_iota(jnp.int32, (nr, PAGE_SIZE), 0) // N_HEADS
        k_idx = lax.broadcasted_iota(jnp.int32, (nr, PAGE_SIZE), 1)
        visit([k_new], [v_new], uu_new_ref[...], k_idx < q_idx)

    @pl.when(s > 0)
    def _():
        visit([r[0].astype(BF16) for r in kp_refs], [r[0].astype(BF16) for r in vp_refs],
              uu_ref[...], None)

    @pl.when(s == pl.num_programs(1) - 1)
    def _():
        sel = jnp.where(own_head, acc_ref[...], 0.0).reshape(tn, N_HEADS, hd)
        o_ref[0] = jnp.sum(sel, axis=1).astype(o_ref.dtype)


def _attn_sample(q, k_new, v_new, cache_k, cache_v, page_table, bias):
    bd, tn, hd = q.shape
    n_pages = page_table.shape[1]
    group = SAMPLE_PAGE_GROUP
    assert n_pages % group == 0
    nr = tn * N_HEADS
    bias_rows = jnp.tile(bias, tn).reshape(nr, 1)

    def page(r):
        return lambda b, s, pt: (pt[b, n_pages - group * jnp.maximum(s, 1) + r], 0, 0)

    per_b = lambda b, s, pt: (b, 0, 0)
    const2 = lambda b, s, pt: (0, 0)
    page_specs = [pl.BlockSpec((1, PAGE_SIZE, hd), page(r)) for r in range(group)]
    grid_spec = pltpu.PrefetchScalarGridSpec(
        num_scalar_prefetch=1,
        grid=(bd, n_pages // group + 1),
        in_specs=[pl.BlockSpec((1, tn, hd), per_b),
                  pl.BlockSpec((nr, 1), const2),
                  pl.BlockSpec((1, k_new.shape[1], hd), per_b),
                  pl.BlockSpec((1, v_new.shape[1], hd), per_b)]
                 + page_specs + page_specs
                 + [pl.BlockSpec((PAGE_SIZE, PAGE_SIZE), const2),
                    pl.BlockSpec((group * PAGE_SIZE, group * PAGE_SIZE), const2)],
        out_specs=pl.BlockSpec((1, tn, hd), per_b),
        scratch_shapes=[pltpu.VMEM((nr, hd), BF16), pltpu.VMEM((nr, LANES), F32),
                        pltpu.VMEM((nr, hd), F32)],
    )
    return pl.pallas_call(
        functools.partial(_attn_sample_kernel, group=group),
        out_shape=jax.ShapeDtypeStruct((bd, tn, hd), BF16),
        grid_spec=grid_spec,
        compiler_params=_params("parallel", "arbitrary"),
        name="attn_sample",
    )(page_table, q, bias_rows, k_new, v_new, *([cache_k] * group), *([cache_v] * group),
      _suffix_ones(PAGE_SIZE), _suffix_ones(group * PAGE_SIZE))


def kernel(x_prompt, x_sample, state_conv, cache_k, cache_v, page_table, norm_mix, norm_ffn, conv_w_pw1, conv_b_pw1, conv_w_dw, conv_b_dw, conv_ln_g, conv_ln_b, conv_w_pw2, conv_b_pw2, kv_norm, w_kv, w_q, w_o, b_logit, ffn_w_gate, ffn_w_up, ffn_w_down, final_norm):
    b, t, d = x_prompt.shape
    bd, tn, _ = x_sample.shape
    hd = N_HEADS * HEAD_DIM
    depth = norm_mix.shape[0]
    n_conv = conv_w_pw1.shape[0]
    n_phys = cache_k.shape[0]
    keep = CONV_WIDTH - 1

    to_time_major = lambda a: a.transpose(1, 0, 2)
    xp = x_prompt.reshape(b * t, d)
    xs = to_time_major(x_sample).reshape(tn * bd, d)
    cache_k = cache_k.reshape(n_phys, PAGE_SIZE, hd)
    cache_v = cache_v.reshape(n_phys, PAGE_SIZE, hd)

    conv_state_p, conv_state_s = [], []
    kv_p = kv_s = None
    y_p = y_s = None
    for l in range(depth):
        last = final_norm if l == depth - 1 else None
        ffn_w = (norm_ffn[l], ffn_w_gate[l].astype(BF16), ffn_w_up[l].astype(BF16),
                 ffn_w_down[l].astype(BF16))
        if l < n_conv:
            w1 = conv_w_pw1[l].astype(BF16)
            tail = (conv_w_dw[l], conv_b_dw[l], conv_ln_g[l], conv_ln_b[l],
                    conv_w_pw2[l].astype(BF16), conv_b_pw2[l])
            u_p = _pw1_glu(xp, norm_mix[l], w1, conv_b_pw1[l]).reshape(b, t, d)
            u_s = _pw1_glu(xs, norm_mix[l], w1, conv_b_pw1[l]).reshape(tn, bd, d)
            conv_state_p.append(u_p[:, t - keep:])
            full_s = jnp.concatenate([to_time_major(state_conv[l]), u_s], axis=0)
            conv_state_s.append(to_time_major(full_s[tn:]))
            xp = _conv_prompt(u_p, xp.reshape(b, t, d), *tail).reshape(b * t, d)
            xs = _conv_sample(full_s, xs.reshape(tn, bd, d), *tail).reshape(tn * bd, d)
            xp = _ffn(xp, *ffn_w, final_gain=last)
            xs = _ffn(xs, *ffn_w, final_gain=last)
        else:
            j = l - n_conv
            if kv_p is None:
                wkv = w_kv.astype(BF16)
                kv_p = _kv_proj(xp, kv_norm, wkv)
                kv_s = _kv_proj(xs, kv_norm, wkv)
                new_rows = [jnp.pad(to_time_major(a.reshape(tn, bd, hd)), ((0, 0), (0, 8 - tn), (0, 0)))
                            for a in kv_s[:2]]
            wq = w_q[j].astype(BF16)
            wo = w_o[j].astype(BF16)
            q_p = _q_proj(xp, norm_mix[l], wq).reshape(b, t, hd)
            q_s = to_time_major(_q_proj(xs, norm_mix[l], wq).reshape(tn, bd, hd))
            bias2 = b_logit[j] * LOG2_E
            a_p = _attn_prompt(q_p, kv_p[2].reshape(b, t, hd), kv_p[3].reshape(b, t, hd), bias2)
            a_s = _attn_sample(q_s, new_rows[0], new_rows[1], cache_k, cache_v, page_table, bias2)
            a_s = to_time_major(a_s).reshape(tn * bd, hd)
            xp = _ffn(xp, *ffn_w, attn=a_p.reshape(b * t, hd), wo=wo, final_gain=last)
            xs = _ffn(xs, *ffn_w, attn=a_s, wo=wo, final_gain=last)

    y_p = xp.reshape(b, t, d)
    y_s = to_time_major(xs.reshape(tn, bd, d))
    heads = lambda a, n0, n1: a.reshape(n0, n1, N_HEADS, HEAD_DIM)
    k_s = to_time_major(kv_s[0].reshape(tn, bd, hd))
    v_s = to_time_major(kv_s[1].reshape(tn, bd, hd))
    return (y_p, y_s, jnp.stack(conv_state_p, axis=0), jnp.stack(conv_state_s, axis=0),
            heads(kv_p[0], b, t), heads(kv_p[1], b, t), heads(k_s, bd, tn), heads(v_s, bd, tn))
```

```python
import functools
import math

import jax
import jax.numpy as jnp
from jax import lax
from jax.experimental import pallas as pl
from jax.experimental.pallas import tpu as pltpu

F32 = jnp.float32
BF16 = jnp.bfloat16

N_HEADS = 16
HEAD_DIM = 64
CONV_WIDTH = 31
PAGE_SIZE = 128
RMS_EPS = 1e-6
LN_EPS = 1e-5
LOG2_E = math.log2(math.e)

LANES = 128
SUBLANES = 8
HEADS_PER_LANE_BLOCK = LANES // HEAD_DIM
CONV_HALO = 32
VMEM_LIMIT = 56 * 1024 * 1024

TOKEN_TILE = 512
CONV_TILE = 256
ATTN_Q_BLOCK = 512
ATTN_K_BLOCK = 256
ATTN_UNROLL = 4
BIAS_TERMS = 3
SAMPLE_PAGE_GROUP = 4


def _params(*sem):
    return pltpu.CompilerParams(dimension_semantics=sem, vmem_limit_bytes=VMEM_LIMIT)


def _const_spec(shape):
    nd = len(shape)
    return pl.BlockSpec(shape, lambda *_: (0,) * nd, pipeline_mode=pl.Buffered(1))


def _rms(x, g):
    ms = jnp.mean(x * x, axis=-1, keepdims=True)
    return x * lax.rsqrt(ms + RMS_EPS) * g


def _silu(x):
    return x * jax.nn.sigmoid(x)


def _dot(a, b):
    return jnp.dot(a, b, preferred_element_type=F32)


def _dot_nt(a, b):
    return lax.dot_general(a, b, (((1,), (1,)), ((), ())), preferred_element_type=F32)


def _pw1_glu_kernel(x_ref, g_ref, w_ref, b_ref, u_ref):
    d = u_ref.shape[-1]
    h = _rms(x_ref[...], g_ref[...]).astype(BF16)
    val = _dot(h, w_ref[:, :d]) + b_ref[:, :d]
    gate = _dot(h, w_ref[:, d:]) + b_ref[:, d:]
    u_ref[...] = val * jax.nn.sigmoid(gate)


def _pw1_glu(x, g, w, b):
    n, d = x.shape
    tm = min(TOKEN_TILE, n)
    return pl.pallas_call(
        _pw1_glu_kernel,
        out_shape=jax.ShapeDtypeStruct((n, d), F32),
        grid=(n // tm,),
        in_specs=[pl.BlockSpec((tm, d), lambda i: (i, 0)),
                  _const_spec((1, d)), _const_spec(w.shape), _const_spec((1, 2 * d))],
        out_specs=pl.BlockSpec((tm, d), lambda i: (i, 0)),
        compiler_params=_params("parallel"),
        name="pw1_glu",
    )(x, g.reshape(1, d), w, b.reshape(1, 2 * d))


def _ffn_kernel(*refs, chunks, pre_proj, final_norm):
    refs = list(refs)
    x_ref = refs.pop(0)
    x = x_ref[...]
    if pre_proj:
        a_ref, wo_ref = refs.pop(0), refs.pop(0)
        x = x + _dot(a_ref[...], wo_ref[...])
    g_ref, wg_ref, wu_ref, wd_ref = refs[:4]
    refs = refs[4:]
    h = _rms(x, g_ref[...]).astype(BF16)
    y = x
    for c0, c1 in chunks:
        gate = _dot(h, wg_ref[:, c0:c1])
        up = _dot(h, wu_ref[:, c0:c1])
        act = (_silu(gate) * up).astype(BF16)
        y = y + _dot(act, wd_ref[c0:c1, :])
    if final_norm:
        fg_ref, o_ref = refs
        o_ref[...] = _rms(y, fg_ref[...])
    else:
        (o_ref,) = refs
        o_ref[...] = y


def _ffn(x, g, wg, wu, wd, attn=None, wo=None, final_gain=None):
    n, d = x.shape
    dff = wg.shape[1]
    tm = min(TOKEN_TILE, n)
    step = 2 * LANES * 3
    chunks = tuple((c, min(c + step, dff)) for c in range(0, dff, step))
    row = lambda i: (i, 0)
    args, specs = [x], [pl.BlockSpec((tm, d), row)]
    if attn is not None:
        args += [attn, wo]
        specs += [pl.BlockSpec((tm, attn.shape[1]), row), _const_spec(wo.shape)]
    args += [g.reshape(1, d), wg, wu, wd]
    specs += [_const_spec((1, d)), _const_spec(wg.shape), _const_spec(wu.shape), _const_spec(wd.shape)]
    if final_gain is not None:
        args.append(final_gain.reshape(1, d))
        specs.append(_const_spec((1, d)))
    return pl.pallas_call(
        functools.partial(_ffn_kernel, chunks=chunks, pre_proj=attn is not None,
                          final_norm=final_gain is not None),
        out_shape=jax.ShapeDtypeStruct((n, d), F32),
        grid=(n // tm,),
        in_specs=specs,
        out_specs=pl.BlockSpec((tm, d), row),
        compiler_params=_params("parallel"),
        name="ffn",
    )(*args)


def _kv_proj_kernel(x_ref, g_ref, w_ref, k_ref, v_ref, kb_ref, vb_ref):
    hd = k_ref.shape[-1]
    h = _rms(x_ref[...], g_ref[...]).astype(BF16)
    k = _dot(h, w_ref[:, :hd])
    v = _dot(h, w_ref[:, hd:])
    k_ref[...] = k
    v_ref[...] = v
    kb_ref[...] = k.astype(BF16)
    vb_ref[...] = v.astype(BF16)


def _kv_proj(x, g, w):
    n, d = x.shape
    hd = w.shape[1] // 2
    tm = min(TOKEN_TILE, n)
    row = lambda i: (i, 0)
    out = pl.BlockSpec((tm, hd), row)
    return pl.pallas_call(
        _kv_proj_kernel,
        out_shape=(jax.ShapeDtypeStruct((n, hd), F32), jax.ShapeDtypeStruct((n, hd), F32),
                   jax.ShapeDtypeStruct((n, hd), BF16), jax.ShapeDtypeStruct((n, hd), BF16)),
        grid=(n // tm,),
        in_specs=[pl.BlockSpec((tm, d), row), _const_spec((1, d)), _const_spec(w.shape)],
        out_specs=(out, out, out, out),
        compiler_params=_params("parallel"),
        name="kv_proj",
    )(x, g.reshape(1, d), w)


def _q_proj_kernel(x_ref, g_ref, w_ref, q_ref, *, scale):
    h = _rms(x_ref[...], g_ref[...]).astype(BF16)
    q_ref[...] = (_dot(h, w_ref[...]) * scale).astype(BF16)


def _q_proj(x, g, w):
    n, d = x.shape
    hd = w.shape[1]
    tm = min(TOKEN_TILE, n)
    row = lambda i: (i, 0)
    return pl.pallas_call(
        functools.partial(_q_proj_kernel, scale=LOG2_E / math.sqrt(HEAD_DIM)),
        out_shape=jax.ShapeDtypeStruct((n, hd), BF16),
        grid=(n // tm,),
        in_specs=[pl.BlockSpec((tm, d), row), _const_spec((1, d)), _const_spec(w.shape)],
        out_specs=pl.BlockSpec((tm, hd), row),
        compiler_params=_params("parallel"),
        name="q_proj",
    )(x, g.reshape(1, d), w)


def _ln_swish_pw2(c, x, lng_ref, lnb_ref, w2_ref, b2_ref):
    mu = jnp.mean(c, axis=-1, keepdims=True)
    xc = c - mu
    var = jnp.mean(xc * xc, axis=-1, keepdims=True)
    y = _silu(xc * lax.rsqrt(var + LN_EPS) * lng_ref[...] + lnb_ref[...])
    return x + _dot(y.astype(BF16), w2_ref[...]) + b2_ref[...]


def _conv_prompt_kernel(u_ref, uh_ref, x_ref, wdw_ref, bdw_ref, lng_ref, lnb_ref, w2_ref, b2_ref,
                        o_ref, win_ref, c_ref):
    i = pl.program_id(1)
    tm, d = c_ref.shape
    n = CONV_HALO + tm
    win_ref[0, 0:CONV_HALO, :] = jnp.where(i == 0, 0.0, uh_ref[0])
    win_ref[0, CONV_HALO:, :] = u_ref[0]
    for s in range(1, SUBLANES):
        win_ref[s, 0:n - SUBLANES, :] = win_ref[0, s:s + n - SUBLANES, :]
    first = CONV_HALO - (CONV_WIDTH - 1)
    rows, cols = 32, 256
    for r in range(0, tm, rows):
        for c in range(0, d, cols):
            acc = jnp.broadcast_to(bdw_ref[:, c:c + cols], (rows, cols))
            for w in range(CONV_WIDTH):
                s = (first + w) % SUBLANES
                base = r + first + w - s
                acc = acc + win_ref[s, base:base + rows, c:c + cols] * wdw_ref[w:w + 1, c:c + cols]
            c_ref[r:r + rows, c:c + cols] = acc
    o_ref[0] = _ln_swish_pw2(c_ref[...], x_ref[0], lng_ref, lnb_ref, w2_ref, b2_ref)


def _conv_prompt(u, x, wdw, bdw, lng, lnb, w2, b2):
    b, t, d = u.shape
    tm = CONV_TILE
    per = tm // CONV_HALO
    tile = pl.BlockSpec((1, tm, d), lambda bi, i: (bi, i, 0))
    halo = pl.BlockSpec((1, CONV_HALO, d), lambda bi, i: (bi, jnp.maximum(i * per - 1, 0), 0))
    vec = _const_spec((1, d))
    return pl.pallas_call(
        _conv_prompt_kernel,
        out_shape=jax.ShapeDtypeStruct((b, t, d), F32),
        grid=(b, t // tm),
        in_specs=[tile, halo, tile, _const_spec(wdw.shape), vec, vec, vec, _const_spec(w2.shape), vec],
        out_specs=tile,
        scratch_shapes=[pltpu.VMEM((SUBLANES, CONV_HALO + tm, d), F32), pltpu.VMEM((tm, d), F32)],
        compiler_params=_params("parallel", "parallel"),
        name="conv_prompt",
    )(u, u, x, wdw, bdw.reshape(1, d), lng.reshape(1, d), lnb.reshape(1, d), w2, b2.reshape(1, d))


def _conv_sample_kernel(full_ref, x_ref, wdw_ref, bdw_ref, lng_ref, lnb_ref, w2_ref, b2_ref, o_ref):
    tn, bs, d = x_ref.shape
    for t in range(tn):
        acc = jnp.broadcast_to(bdw_ref[...], (bs, d))
        for w in range(CONV_WIDTH):
            acc = acc + full_ref[t + w] * wdw_ref[w:w + 1, :]
        o_ref[t] = _ln_swish_pw2(acc, x_ref[t], lng_ref, lnb_ref, w2_ref, b2_ref)


def _conv_sample(full, x, wdw, bdw, lng, lnb, w2, b2):
    tn, nb, d = x.shape
    bs = 32
    vec = _const_spec((1, d))
    return pl.pallas_call(
        _conv_sample_kernel,
        out_shape=jax.ShapeDtypeStruct((tn, nb, d), F32),
        grid=(nb // bs,),
        in_specs=[pl.BlockSpec((full.shape[0], bs, d), lambda i: (0, i, 0)),
                  pl.BlockSpec((tn, bs, d), lambda i: (0, i, 0)),
                  _const_spec(wdw.shape), vec, vec, vec, _const_spec(w2.shape), vec],
        out_specs=pl.BlockSpec((tn, bs, d), lambda i: (0, i, 0)),
        compiler_params=_params("parallel"),
        name="conv_sample",
    )(full, x, wdw, bdw.reshape(1, d), lng.reshape(1, d), lnb.reshape(1, d), w2, b2.reshape(1, d))


def _suffix_ones(n):
    j = lax.broadcasted_iota(jnp.int32, (n, n), 0)
    s = lax.broadcasted_iota(jnp.int32, (n, n), 1)
    return (j >= s).astype(BF16)


def _stick_tile(z, carry, u_incl, mask):
    neg_abs = lax.bitcast_convert_type(
        lax.bitcast_convert_type(z, jnp.uint32) | jnp.uint32(0x80000000), F32)
    sp = jnp.maximum(z, 0.0) + jnp.log2(1.0 + jnp.exp2(neg_abs))
    if mask is not None:
        sp = jnp.where(mask, sp, 0.0)
    s_incl = _dot(sp.astype(BF16), u_incl)
    reps = z.shape[1] // LANES
    a = jnp.exp2(z - s_incl - jnp.concatenate([carry] * reps, axis=1))
    if mask is not None:
        a = jnp.where(mask, a, 0.0)
    total = s_incl[:, :1]
    return a.astype(BF16), carry + jnp.broadcast_to(total, carry.shape)


def _attn_prompt_kernel(bias_ref, q_ref, k_ref, v_ref, u_ref, o_ref, qs_ref, carry_ref, acc_ref,
                        *, qb, kb):
    p = pl.program_id(1)
    i = pl.program_id(2)
    nh = HEADS_PER_LANE_BLOCK
    q2 = q_ref[0]
    lane = lax.broadcasted_iota(jnp.int32, (qb, LANES), 1)
    for hh in range(nh):
        in_head = (lane >= hh * HEAD_DIM) & (lane < (hh + 1) * HEAD_DIM)
        qs_ref[hh * qb:(hh + 1) * qb, :LANES] = jnp.where(in_head, q2, jnp.zeros_like(q2))
        rest = jnp.full((qb, LANES), bias_ref[nh * p + hh], F32)
        ext = jnp.zeros((qb, LANES), F32)
        for term in range(BIAS_TERMS):
            part = rest.astype(BF16).astype(F32)
            rest = rest - part
            ext = jnp.where(lane == term, part, ext)
        qs_ref[hh * qb:(hh + 1) * qb, LANES:] = ext.astype(BF16)
    carry_ref[...] = jnp.zeros_like(carry_ref)
    acc_ref[...] = jnp.zeros_like(acc_ref)
    k_lane = lax.broadcasted_iota(jnp.int32, (kb, LANES), 1)
    ones_lanes = jnp.where(k_lane < BIAS_TERMS, 1.0, 0.0).astype(BF16)

    def tile(j, masked):
        start = pl.multiple_of(j * kb, kb)
        k_blk = jnp.concatenate([k_ref[0, pl.ds(start, kb), :], ones_lanes], axis=1)
        v_blk = v_ref[0, pl.ds(start, kb), :]
        z = _dot_nt(qs_ref[...], k_blk)
        mask = None
        if masked:
            row = lax.broadcasted_iota(jnp.int32, (nh * qb, kb), 0)
            q_pos = i * qb + row % qb
            mask = start + lax.broadcasted_iota(jnp.int32, (nh * qb, kb), 1) < q_pos
        a, carry = _stick_tile(z, carry_ref[...], u_ref[...], mask)
        carry_ref[...] = carry
        acc_ref[...] += _dot(a, v_blk)

    n_diag = qb // kb
    first = i * n_diag
    for s in range(n_diag):
        tile(first + n_diag - 1 - s, True)

    assert ATTN_UNROLL == 2 * n_diag
    odd = first % ATTN_UNROLL

    @pl.when(odd > 0)
    def _():
        for s in range(n_diag):
            tile(first - 1 - s, False)

    @pl.loop(0, first // ATTN_UNROLL)
    def _(g):
        for s in range(ATTN_UNROLL):
            tile(first - odd - 1 - g * ATTN_UNROLL - s, False)

    out = jnp.zeros((qb, LANES), F32)
    for hh in range(nh):
        in_head = (lane >= hh * HEAD_DIM) & (lane < (hh + 1) * HEAD_DIM)
        out = jnp.where(in_head, acc_ref[hh * qb:(hh + 1) * qb, :], out)
    o_ref[0] = out.astype(o_ref.dtype)


def _attn_prompt(q, k, v, bias):
    b, t, hd = q.shape
    qb, kb = ATTN_Q_BLOCK, ATTN_K_BLOCK
    rows = HEADS_PER_LANE_BLOCK * qb
    kv_spec = pl.BlockSpec((1, t, LANES), lambda bi, p, i: (bi, 0, p))
    q_spec = pl.BlockSpec((1, qb, LANES), lambda bi, p, i: (bi, i, p))
    return pl.pallas_call(
        functools.partial(_attn_prompt_kernel, qb=qb, kb=kb),
        out_shape=jax.ShapeDtypeStruct((b, t, hd), BF16),
        grid=(b, hd // LANES, t // qb),
        in_specs=[pl.BlockSpec(memory_space=pltpu.SMEM), q_spec, kv_spec, kv_spec,
                  _const_spec((kb, kb))],
        out_specs=q_spec,
        scratch_shapes=[pltpu.VMEM((rows, 2 * LANES), BF16), pltpu.VMEM((rows, LANES), F32),
                        pltpu.VMEM((rows, LANES), F32)],
        compiler_params=_params("parallel", "parallel", "arbitrary"),
        name="attn_prompt",
    )(bias, q, k, v, _suffix_ones(kb))


def _attn_sample_kernel(pt_ref, q_ref, bias_ref, kn_ref, vn_ref, *refs, group):
    del pt_ref
    kp_refs, vp_refs = refs[:group], refs[group:2 * group]
    uu_new_ref, uu_ref, o_ref, qx_ref, carry_ref, acc_ref = refs[2 * group:]
    s = pl.program_id(1)
    tn = q_ref.shape[1]
    nr = tn * N_HEADS
    hd = N_HEADS * HEAD_DIM
    row = lax.broadcasted_iota(jnp.int32, (nr, hd), 0)
    lane_head = lax.broadcasted_iota(jnp.int32, (nr, hd), 1) // HEAD_DIM
    own_head = (row % N_HEADS) == lane_head

    def visit(k_pages, v_pages, uu, mask):
        z = jnp.concatenate([_dot_nt(qx_ref[...], kp) for kp in k_pages], axis=1) + bias_ref[...]
        a, carry = _stick_tile(z, carry_ref[...], uu, mask)
        carry_ref[...] = carry
        acc = acc_ref[...]
        for r, vp in enumerate(v_pages):
            acc = acc + _dot(a[:, r * PAGE_SIZE:(r + 1) * PAGE_SIZE], vp)
        acc_ref[...] = acc

    @pl.when(s == 0)
    def _():
        q4 = q_ref[0].astype(F32)
        rows = jnp.concatenate(
            [jnp.broadcast_to(q4[i:i + 1], (N_HEADS, hd)) for i in range(tn)], axis=0)
        qx_ref[...] = jnp.where(own_head, rows, 0.0).astype(BF16)
        carry_ref[...] = jnp.zeros_like(carry_ref)
        acc_ref[...] = jnp.zeros_like(acc_ref)
        pad = jnp.zeros((PAGE_SIZE - kn_ref.shape[1], hd), F32)
        k_new = jnp.concatenate([kn_ref[0], pad], axis=0).astype(BF16)
        v_new = jnp.concatenate([vn_ref[0], pad], axis=0).astype(BF16)
        q_idx = lax.broadcasted_iota(jnp.int32, (nr, PAGE_SIZE), 0) // N_HEADS
        k_idx = lax.broadcasted_iota(jnp.int32, (nr, PAGE_SIZE), 1)
        visit([k_new], [v_new], uu_new_ref[...], k_idx < q_idx)

    @pl.when(s > 0)
    def _():
        visit([r[0] for r in kp_refs], [r[0] for r in vp_refs],
              uu_ref[...], None)

    @pl.when(s == pl.num_programs(1) - 1)
    def _():
        sel = jnp.where(own_head, acc_ref[...], 0.0).reshape(tn, N_HEADS, hd)
        o_ref[0] = jnp.sum(sel, axis=1).astype(o_ref.dtype)


def _attn_sample(q, k_new, v_new, cache_k, cache_v, page_table, bias):
    bd, tn, hd = q.shape
    n_pages = page_table.shape[1]
    group = SAMPLE_PAGE_GROUP
    assert n_pages % group == 0
    nr = tn * N_HEADS
    bias_rows = jnp.tile(bias, tn).reshape(nr, 1)

    def page(r):
        return lambda b, s, pt: (pt[b, n_pages - group * jnp.maximum(s, 1) + r], 0, 0)

    per_b = lambda b, s, pt: (b, 0, 0)
    const2 = lambda b, s, pt: (0, 0)
    page_specs = [pl.BlockSpec((1, PAGE_SIZE, hd), page(r)) for r in range(group)]
    grid_spec = pltpu.PrefetchScalarGridSpec(
        num_scalar_prefetch=1,
        grid=(bd, n_pages // group + 1),
        in_specs=[pl.BlockSpec((1, tn, hd), per_b),
                  pl.BlockSpec((nr, 1), const2),
                  pl.BlockSpec((1, k_new.shape[1], hd), per_b),
                  pl.BlockSpec((1, v_new.shape[1], hd), per_b)]
                 + page_specs + page_specs
                 + [pl.BlockSpec((PAGE_SIZE, PAGE_SIZE), const2),
                    pl.BlockSpec((group * PAGE_SIZE, group * PAGE_SIZE), const2)],
        out_specs=pl.BlockSpec((1, tn, hd), per_b),
        scratch_shapes=[pltpu.VMEM((nr, hd), BF16), pltpu.VMEM((nr, LANES), F32),
                        pltpu.VMEM((nr, hd), F32)],
    )
    return pl.pallas_call(
        functools.partial(_attn_sample_kernel, group=group),
        out_shape=jax.ShapeDtypeStruct((bd, tn, hd), BF16),
        grid_spec=grid_spec,
        compiler_params=_params("parallel", "arbitrary"),
        name="attn_sample",
    )(page_table, q, bias_rows, k_new, v_new, *([cache_k] * group), *([cache_v] * group),
      _suffix_ones(PAGE_SIZE), _suffix_ones(group * PAGE_SIZE))


def kernel(x_prompt, x_sample, state_conv, cache_k, cache_v, page_table, norm_mix, norm_ffn, conv_w_pw1, conv_b_pw1, conv_w_dw, conv_b_dw, conv_ln_g, conv_ln_b, conv_w_pw2, conv_b_pw2, kv_norm, w_kv, w_q, w_o, b_logit, ffn_w_gate, ffn_w_up, ffn_w_down, final_norm):
    b, t, d = x_prompt.shape
    bd, tn, _ = x_sample.shape
    hd = N_HEADS * HEAD_DIM
    depth = norm_mix.shape[0]
    n_conv = conv_w_pw1.shape[0]
    n_phys = cache_k.shape[0]
    keep = CONV_WIDTH - 1

    to_time_major = lambda a: a.transpose(1, 0, 2)
    xp = x_prompt.reshape(b * t, d)
    xs = to_time_major(x_sample).reshape(tn * bd, d)
    cache_k = cache_k.reshape(n_phys, PAGE_SIZE, hd).astype(BF16)
    cache_v = cache_v.reshape(n_phys, PAGE_SIZE, hd).astype(BF16)

    conv_state_p, conv_state_s = [], []
    kv_p = kv_s = None
    y_p = y_s = None
    for l in range(depth):
        last = final_norm if l == depth - 1 else None
        ffn_w = (norm_ffn[l], ffn_w_gate[l].astype(BF16), ffn_w_up[l].astype(BF16),
                 ffn_w_down[l].astype(BF16))
        if l < n_conv:
            w1 = conv_w_pw1[l].astype(BF16)
            tail = (conv_w_dw[l], conv_b_dw[l], conv_ln_g[l], conv_ln_b[l],
                    conv_w_pw2[l].astype(BF16), conv_b_pw2[l])
            u_p = _pw1_glu(xp, norm_mix[l], w1, conv_b_pw1[l]).reshape(b, t, d)
            u_s = _pw1_glu(xs, norm_mix[l], w1, conv_b_pw1[l]).reshape(tn, bd, d)
            conv_state_p.append(u_p[:, t - keep:])
            full_s = jnp.concatenate([to_time_major(state_conv[l]), u_s], axis=0)
            conv_state_s.append(to_time_major(full_s[tn:]))
            xp = _conv_prompt(u_p, xp.reshape(b, t, d), *tail).reshape(b * t, d)
            xs = _conv_sample(full_s, xs.reshape(tn, bd, d), *tail).reshape(tn * bd, d)
            xp = _ffn(xp, *ffn_w, final_gain=last)
            xs = _ffn(xs, *ffn_w, final_gain=last)
        else:
            j = l - n_conv
            if kv_p is None:
                wkv = w_kv.astype(BF16)
                kv_p = _kv_proj(xp, kv_norm, wkv)
                kv_s = _kv_proj(xs, kv_norm, wkv)
                new_rows = [jnp.pad(to_time_major(a.reshape(tn, bd, hd)), ((0, 0), (0, 8 - tn), (0, 0)))
                            for a in kv_s[:2]]
            wq = w_q[j].astype(BF16)
            wo = w_o[j].astype(BF16)
            q_p = _q_proj(xp, norm_mix[l], wq).reshape(b, t, hd)
            q_s = to_time_major(_q_proj(xs, norm_mix[l], wq).reshape(tn, bd, hd))
            bias2 = b_logit[j] * LOG2_E
            a_p = _attn_prompt(q_p, kv_p[2].reshape(b, t, hd), kv_p[3].reshape(b, t, hd), bias2)
            a_s = _attn_sample(q_s, new_rows[0], new_rows[1], cache_k, cache_v, page_table, bias2)
            a_s = to_time_major(a_s).reshape(tn * bd, hd)
            xp = _ffn(xp, *ffn_w, attn=a_p.reshape(b * t, hd), wo=wo, final_gain=last)
            xs = _ffn(xs, *ffn_w, attn=a_s, wo=wo, final_gain=last)

    y_p = xp.reshape(b, t, d)
    y_s = to_time_major(xs.reshape(tn, bd, d))
    heads = lambda a, n0, n1: a.reshape(n0, n1, N_HEADS, HEAD_DIM)
    k_s = to_time_major(kv_s[0].reshape(tn, bd, hd))
    v_s = to_time_major(kv_s[1].reshape(tn, bd, hd))
    return (y_p, y_s, jnp.stack(conv_state_p, axis=0), jnp.stack(conv_state_s, axis=0),
            heads(kv_p[0], b, t), heads(kv_p[1], b, t), heads(k_s, bd, tn), heads(v_s, bd, tn))
```

```python
import functools
import math

import jax
import jax.numpy as jnp
from jax import lax
from jax.experimental import pallas as pl
from jax.experimental.pallas import tpu as pltpu

F32 = jnp.float32
BF16 = jnp.bfloat16

N_HEADS = 16
HEAD_DIM = 64
CONV_WIDTH = 31
PAGE_SIZE = 128
RMS_EPS = 1e-6
LN_EPS = 1e-5
LOG2_E = math.log2(math.e)

LANES = 128
SUBLANES = 8
HEADS_PER_LANE_BLOCK = LANES // HEAD_DIM
CONV_HALO = 32
VMEM_LIMIT = 56 * 1024 * 1024

TOKEN_TILE = 512
CONV_TILE = 256
ATTN_Q_BLOCK = 512
ATTN_K_BLOCK = 256
ATTN_UNROLL = 4
BIAS_TERMS = 3
SAMPLE_KEY_BLOCK = 512


def _params(*sem):
    return pltpu.CompilerParams(dimension_semantics=sem, vmem_limit_bytes=VMEM_LIMIT)


def _const_spec(shape):
    nd = len(shape)
    return pl.BlockSpec(shape, lambda *_: (0,) * nd, pipeline_mode=pl.Buffered(1))


def _rms(x, g):
    ms = jnp.mean(x * x, axis=-1, keepdims=True)
    return x * lax.rsqrt(ms + RMS_EPS) * g


def _silu(x):
    return x * jax.nn.sigmoid(x)


def _dot(a, b):
    return jnp.dot(a, b, preferred_element_type=F32)


def _dot_nt(a, b):
    return lax.dot_general(a, b, (((1,), (1,)), ((), ())), preferred_element_type=F32)


def _pw1_glu_kernel(x_ref, g_ref, w_ref, b_ref, u_ref):
    d = u_ref.shape[-1]
    h = _rms(x_ref[...], g_ref[...]).astype(BF16)
    val = _dot(h, w_ref[:, :d]) + b_ref[:, :d]
    gate = _dot(h, w_ref[:, d:]) + b_ref[:, d:]
    u_ref[...] = val * jax.nn.sigmoid(gate)


def _pw1_glu(x, g, w, b):
    n, d = x.shape
    tm = min(TOKEN_TILE, n)
    return pl.pallas_call(
        _pw1_glu_kernel,
        out_shape=jax.ShapeDtypeStruct((n, d), F32),
        grid=(n // tm,),
        in_specs=[pl.BlockSpec((tm, d), lambda i: (i, 0)),
                  _const_spec((1, d)), _const_spec(w.shape), _const_spec((1, 2 * d))],
        out_specs=pl.BlockSpec((tm, d), lambda i: (i, 0)),
        compiler_params=_params("parallel"),
        name="pw1_glu",
    )(x, g.reshape(1, d), w, b.reshape(1, 2 * d))


def _ffn_kernel(*refs, chunks, pre_proj, final_norm):
    refs = list(refs)
    x_ref = refs.pop(0)
    x = x_ref[...]
    if pre_proj:
        a_ref, wo_ref = refs.pop(0), refs.pop(0)
        x = x + _dot(a_ref[...], wo_ref[...])
    g_ref, wg_ref, wu_ref, wd_ref = refs[:4]
    refs = refs[4:]
    h = _rms(x, g_ref[...]).astype(BF16)
    y = x
    for c0, c1 in chunks:
        gate = _dot(h, wg_ref[:, c0:c1])
        up = _dot(h, wu_ref[:, c0:c1])
        act = (_silu(gate) * up).astype(BF16)
        y = y + _dot(act, wd_ref[c0:c1, :])
    if final_norm:
        fg_ref, o_ref = refs
        o_ref[...] = _rms(y, fg_ref[...])
    else:
        (o_ref,) = refs
        o_ref[...] = y


def _ffn(x, g, wg, wu, wd, attn=None, wo=None, final_gain=None):
    n, d = x.shape
    dff = wg.shape[1]
    tm = min(TOKEN_TILE, n)
    step = 2 * LANES * 3
    chunks = tuple((c, min(c + step, dff)) for c in range(0, dff, step))
    row = lambda i: (i, 0)
    args, specs = [x], [pl.BlockSpec((tm, d), row)]
    if attn is not None:
        args += [attn, wo]
        specs += [pl.BlockSpec((tm, attn.shape[1]), row), _const_spec(wo.shape)]
    args += [g.reshape(1, d), wg, wu, wd]
    specs += [_const_spec((1, d)), _const_spec(wg.shape), _const_spec(wu.shape), _const_spec(wd.shape)]
    if final_gain is not None:
        args.append(final_gain.reshape(1, d))
        specs.append(_const_spec((1, d)))
    return pl.pallas_call(
        functools.partial(_ffn_kernel, chunks=chunks, pre_proj=attn is not None,
                          final_norm=final_gain is not None),
        out_shape=jax.ShapeDtypeStruct((n, d), F32),
        grid=(n // tm,),
        in_specs=specs,
        out_specs=pl.BlockSpec((tm, d), row),
        compiler_params=_params("parallel"),
        name="ffn",
    )(*args)


def _kv_proj_kernel(x_ref, g_ref, w_ref, k_ref, v_ref, kb_ref, vb_ref):
    hd = k_ref.shape[-1]
    h = _rms(x_ref[...], g_ref[...]).astype(BF16)
    k = _dot(h, w_ref[:, :hd])
    v = _dot(h, w_ref[:, hd:])
    k_ref[...] = k
    v_ref[...] = v
    kb_ref[...] = k.astype(BF16)
    vb_ref[...] = v.astype(BF16)


def _kv_proj(x, g, w):
    n, d = x.shape
    hd = w.shape[1] // 2
    tm = min(TOKEN_TILE, n)
    row = lambda i: (i, 0)
    out = pl.BlockSpec((tm, hd), row)
    return pl.pallas_call(
        _kv_proj_kernel,
        out_shape=(jax.ShapeDtypeStruct((n, hd), F32), jax.ShapeDtypeStruct((n, hd), F32),
                   jax.ShapeDtypeStruct((n, hd), BF16), jax.ShapeDtypeStruct((n, hd), BF16)),
        grid=(n // tm,),
        in_specs=[pl.BlockSpec((tm, d), row), _const_spec((1, d)), _const_spec(w.shape)],
        out_specs=(out, out, out, out),
        compiler_params=_params("parallel"),
        name="kv_proj",
    )(x, g.reshape(1, d), w)


def _q_proj_kernel(x_ref, g_ref, w_ref, q_ref, *, scale):
    h = _rms(x_ref[...], g_ref[...]).astype(BF16)
    q_ref[...] = (_dot(h, w_ref[...]) * scale).astype(BF16)


def _q_proj(x, g, w):
    n, d = x.shape
    hd = w.shape[1]
    tm = min(TOKEN_TILE, n)
    row = lambda i: (i, 0)
    return pl.pallas_call(
        functools.partial(_q_proj_kernel, scale=LOG2_E / math.sqrt(HEAD_DIM)),
        out_shape=jax.ShapeDtypeStruct((n, hd), BF16),
        grid=(n // tm,),
        in_specs=[pl.BlockSpec((tm, d), row), _const_spec((1, d)), _const_spec(w.shape)],
        out_specs=pl.BlockSpec((tm, hd), row),
        compiler_params=_params("parallel"),
        name="q_proj",
    )(x, g.reshape(1, d), w)


def _ln_swish_pw2(c, x, lng_ref, lnb_ref, w2_ref, b2_ref):
    mu = jnp.mean(c, axis=-1, keepdims=True)
    xc = c - mu
    var = jnp.mean(xc * xc, axis=-1, keepdims=True)
    y = _silu(xc * lax.rsqrt(var + LN_EPS) * lng_ref[...] + lnb_ref[...])
    return x + _dot(y.astype(BF16), w2_ref[...]) + b2_ref[...]


def _conv_prompt_kernel(u_ref, uh_ref, x_ref, wdw_ref, bdw_ref, lng_ref, lnb_ref, w2_ref, b2_ref,
                        o_ref, win_ref, c_ref):
    i = pl.program_id(1)
    tm, d = c_ref.shape
    n = CONV_HALO + tm
    win_ref[0, 0:CONV_HALO, :] = jnp.where(i == 0, 0.0, uh_ref[0])
    win_ref[0, CONV_HALO:, :] = u_ref[0]
    for s in range(1, SUBLANES):
        win_ref[s, 0:n - SUBLANES, :] = win_ref[0, s:s + n - SUBLANES, :]
    first = CONV_HALO - (CONV_WIDTH - 1)
    rows, cols = 32, 256
    for r in range(0, tm, rows):
        for c in range(0, d, cols):
            acc = jnp.broadcast_to(bdw_ref[:, c:c + cols], (rows, cols))
            for w in range(CONV_WIDTH):
                s = (first + w) % SUBLANES
                base = r + first + w - s
                acc = acc + win_ref[s, base:base + rows, c:c + cols] * wdw_ref[w:w + 1, c:c + cols]
            c_ref[r:r + rows, c:c + cols] = acc
    o_ref[0] = _ln_swish_pw2(c_ref[...], x_ref[0], lng_ref, lnb_ref, w2_ref, b2_ref)


def _conv_prompt(u, x, wdw, bdw, lng, lnb, w2, b2):
    b, t, d = u.shape
    tm = CONV_TILE
    per = tm // CONV_HALO
    tile = pl.BlockSpec((1, tm, d), lambda bi, i: (bi, i, 0))
    halo = pl.BlockSpec((1, CONV_HALO, d), lambda bi, i: (bi, jnp.maximum(i * per - 1, 0), 0))
    vec = _const_spec((1, d))
    return pl.pallas_call(
        _conv_prompt_kernel,
        out_shape=jax.ShapeDtypeStruct((b, t, d), F32),
        grid=(b, t // tm),
        in_specs=[tile, halo, tile, _const_spec(wdw.shape), vec, vec, vec, _const_spec(w2.shape), vec],
        out_specs=tile,
        scratch_shapes=[pltpu.VMEM((SUBLANES, CONV_HALO + tm, d), F32), pltpu.VMEM((tm, d), F32)],
        compiler_params=_params("parallel", "parallel"),
        name="conv_prompt",
    )(u, u, x, wdw, bdw.reshape(1, d), lng.reshape(1, d), lnb.reshape(1, d), w2, b2.reshape(1, d))


def _conv_sample_kernel(full_ref, x_ref, wdw_ref, bdw_ref, lng_ref, lnb_ref, w2_ref, b2_ref, o_ref):
    tn, bs, d = x_ref.shape
    for t in range(tn):
        acc = jnp.broadcast_to(bdw_ref[...], (bs, d))
        for w in range(CONV_WIDTH):
            acc = acc + full_ref[t + w] * wdw_ref[w:w + 1, :]
        o_ref[t] = _ln_swish_pw2(acc, x_ref[t], lng_ref, lnb_ref, w2_ref, b2_ref)


def _conv_sample(full, x, wdw, bdw, lng, lnb, w2, b2):
    tn, nb, d = x.shape
    bs = 32
    vec = _const_spec((1, d))
    return pl.pallas_call(
        _conv_sample_kernel,
        out_shape=jax.ShapeDtypeStruct((tn, nb, d), F32),
        grid=(nb // bs,),
        in_specs=[pl.BlockSpec((full.shape[0], bs, d), lambda i: (0, i, 0)),
                  pl.BlockSpec((tn, bs, d), lambda i: (0, i, 0)),
                  _const_spec(wdw.shape), vec, vec, vec, _const_spec(w2.shape), vec],
        out_specs=pl.BlockSpec((tn, bs, d), lambda i: (0, i, 0)),
        compiler_params=_params("parallel"),
        name="conv_sample",
    )(full, x, wdw, bdw.reshape(1, d), lng.reshape(1, d), lnb.reshape(1, d), w2, b2.reshape(1, d))


def _suffix_ones(n):
    j = lax.broadcasted_iota(jnp.int32, (n, n), 0)
    s = lax.broadcasted_iota(jnp.int32, (n, n), 1)
    return (j >= s).astype(BF16)


def _stick_tile(z, carry, u_incl, mask):
    neg_abs = lax.bitcast_convert_type(
        lax.bitcast_convert_type(z, jnp.uint32) | jnp.uint32(0x80000000), F32)
    sp = jnp.maximum(z, 0.0) + jnp.log2(1.0 + jnp.exp2(neg_abs))
    if mask is not None:
        sp = jnp.where(mask, sp, 0.0)
    s_incl = _dot(sp.astype(BF16), u_incl)
    reps = z.shape[1] // LANES
    a = jnp.exp2(z - s_incl - jnp.concatenate([carry] * reps, axis=1))
    if mask is not None:
        a = jnp.where(mask, a, 0.0)
    total = s_incl[:, :1]
    return a.astype(BF16), carry + jnp.broadcast_to(total, carry.shape)


def _attn_prompt_kernel(bias_ref, q_ref, k_ref, v_ref, u_ref, o_ref, qs_ref, carry_ref, acc_ref,
                        *, qb, kb):
    p = pl.program_id(1)
    i = pl.program_id(2)
    nh = HEADS_PER_LANE_BLOCK
    q2 = q_ref[0]
    lane = lax.broadcasted_iota(jnp.int32, (qb, LANES), 1)
    for hh in range(nh):
        in_head = (lane >= hh * HEAD_DIM) & (lane < (hh + 1) * HEAD_DIM)
        qs_ref[hh * qb:(hh + 1) * qb, :LANES] = jnp.where(in_head, q2, jnp.zeros_like(q2))
        rest = jnp.full((qb, LANES), bias_ref[nh * p + hh], F32)
        ext = jnp.zeros((qb, LANES), F32)
        for term in range(BIAS_TERMS):
            part = rest.astype(BF16).astype(F32)
            rest = rest - part
            ext = jnp.where(lane == term, part, ext)
        qs_ref[hh * qb:(hh + 1) * qb, LANES:] = ext.astype(BF16)
    carry_ref[...] = jnp.zeros_like(carry_ref)
    acc_ref[...] = jnp.zeros_like(acc_ref)
    k_lane = lax.broadcasted_iota(jnp.int32, (kb, LANES), 1)
    ones_lanes = jnp.where(k_lane < BIAS_TERMS, 1.0, 0.0).astype(BF16)

    def tile(j, masked):
        start = pl.multiple_of(j * kb, kb)
        k_blk = jnp.concatenate([k_ref[0, pl.ds(start, kb), :], ones_lanes], axis=1)
        v_blk = v_ref[0, pl.ds(start, kb), :]
        z = _dot_nt(qs_ref[...], k_blk)
        mask = None
        if masked:
            row = lax.broadcasted_iota(jnp.int32, (nh * qb, kb), 0)
            q_pos = i * qb + row % qb
            mask = start + lax.broadcasted_iota(jnp.int32, (nh * qb, kb), 1) < q_pos
        a, carry = _stick_tile(z, carry_ref[...], u_ref[...], mask)
        carry_ref[...] = carry
        acc_ref[...] += _dot(a, v_blk)

    n_diag = qb // kb
    first = i * n_diag
    for s in range(n_diag):
        tile(first + n_diag - 1 - s, True)

    assert ATTN_UNROLL == 2 * n_diag
    odd = first % ATTN_UNROLL

    @pl.when(odd > 0)
    def _():
        for s in range(n_diag):
            tile(first - 1 - s, False)

    @pl.loop(0, first // ATTN_UNROLL)
    def _(g):
        for s in range(ATTN_UNROLL):
            tile(first - odd - 1 - g * ATTN_UNROLL - s, False)

    out = jnp.zeros((qb, LANES), F32)
    for hh in range(nh):
        in_head = (lane >= hh * HEAD_DIM) & (lane < (hh + 1) * HEAD_DIM)
        out = jnp.where(in_head, acc_ref[hh * qb:(hh + 1) * qb, :], out)
    o_ref[0] = out.astype(o_ref.dtype)


def _attn_prompt(q, k, v, bias):
    b, t, hd = q.shape
    qb, kb = ATTN_Q_BLOCK, ATTN_K_BLOCK
    rows = HEADS_PER_LANE_BLOCK * qb
    kv_spec = pl.BlockSpec((1, t, LANES), lambda bi, p, i: (bi, 0, p))
    q_spec = pl.BlockSpec((1, qb, LANES), lambda bi, p, i: (bi, i, p))
    return pl.pallas_call(
        functools.partial(_attn_prompt_kernel, qb=qb, kb=kb),
        out_shape=jax.ShapeDtypeStruct((b, t, hd), BF16),
        grid=(b, hd // LANES, t // qb),
        in_specs=[pl.BlockSpec(memory_space=pltpu.SMEM), q_spec, kv_spec, kv_spec,
                  _const_spec((kb, kb))],
        out_specs=q_spec,
        scratch_shapes=[pltpu.VMEM((rows, 2 * LANES), BF16), pltpu.VMEM((rows, LANES), F32),
                        pltpu.VMEM((rows, LANES), F32)],
        compiler_params=_params("parallel", "parallel", "arbitrary"),
        name="attn_prompt",
    )(bias, q, k, v, _suffix_ones(kb))


def _gather_pages_kernel(pt_ref, k_ref, v_ref, ko_ref, vo_ref):
    del pt_ref
    for src, dst in ((k_ref, ko_ref), (v_ref, vo_ref)):
        for h in range(N_HEADS):
            rows = src[0, pl.ds(h, PAGE_SIZE, stride=N_HEADS), :]
            dst[0, :, h * HEAD_DIM:(h + 1) * HEAD_DIM] = rows.astype(dst.dtype)


def _gather_pages(cache_k, cache_v, page_table):
    n_phys = cache_k.shape[0]
    bd, n_pages = page_table.shape
    hd = N_HEADS * HEAD_DIM
    src = pl.BlockSpec((1, PAGE_SIZE * N_HEADS, HEAD_DIM), lambda b, p, pt: (pt[b, p], 0, 0))
    dst = pl.BlockSpec((1, PAGE_SIZE, hd), lambda b, p, pt: (b, p, 0))
    out = jax.ShapeDtypeStruct((bd, n_pages * PAGE_SIZE, hd), BF16)
    flat = lambda c: c.reshape(n_phys, PAGE_SIZE * N_HEADS, HEAD_DIM)
    return pl.pallas_call(
        _gather_pages_kernel,
        out_shape=(out, out),
        grid_spec=pltpu.PrefetchScalarGridSpec(
            num_scalar_prefetch=1, grid=(bd, n_pages), in_specs=[src, src], out_specs=(dst, dst)),
        compiler_params=_params("parallel", "parallel"),
        name="gather_pages",
    )(page_table, flat(cache_k), flat(cache_v))


def _attn_sample_kernel(q_ref, bias_ref, kn_ref, vn_ref, kp_ref, vp_ref, u_new_ref, u_ref, o_ref,
                        qx_ref, carry_ref, acc_ref):
    s = pl.program_id(1)
    tn = q_ref.shape[1]
    nr = tn * N_HEADS
    hd = N_HEADS * HEAD_DIM
    row = lax.broadcasted_iota(jnp.int32, (nr, hd), 0)
    lane_head = lax.broadcasted_iota(jnp.int32, (nr, hd), 1) // HEAD_DIM
    own_head = (row % N_HEADS) == lane_head

    def visit(k_blk, v_blk, u_incl, mask):
        z = _dot_nt(qx_ref[...], k_blk) + bias_ref[...]
        a, carry = _stick_tile(z, carry_ref[...], u_incl, mask)
        carry_ref[...] = carry
        acc_ref[...] += _dot(a, v_blk)

    @pl.when(s == 0)
    def _():
        q4 = q_ref[0].astype(F32)
        rows = jnp.concatenate(
            [jnp.broadcast_to(q4[i:i + 1], (N_HEADS, hd)) for i in range(tn)], axis=0)
        qx_ref[...] = jnp.where(own_head, rows, 0.0).astype(BF16)
        carry_ref[...] = jnp.zeros_like(carry_ref)
        acc_ref[...] = jnp.zeros_like(acc_ref)
        pad = jnp.zeros((PAGE_SIZE - kn_ref.shape[1], hd), F32)
        k_new = jnp.concatenate([kn_ref[0], pad], axis=0).astype(BF16)
        v_new = jnp.concatenate([vn_ref[0], pad], axis=0).astype(BF16)
        q_idx = lax.broadcasted_iota(jnp.int32, (nr, PAGE_SIZE), 0) // N_HEADS
        k_idx = lax.broadcasted_iota(jnp.int32, (nr, PAGE_SIZE), 1)
        visit(k_new, v_new, u_new_ref[...], k_idx < q_idx)

    @pl.when(s > 0)
    def _():
        visit(kp_ref[0], vp_ref[0], u_ref[...], None)

    @pl.when(s == pl.num_programs(1) - 1)
    def _():
        sel = jnp.where(own_head, acc_ref[...], 0.0).reshape(tn, N_HEADS, hd)
        o_ref[0] = jnp.sum(sel, axis=1).astype(o_ref.dtype)


def _attn_sample(q, k_new, v_new, k_past, v_past, bias):
    bd, tn, hd = q.shape
    past = k_past.shape[1]
    kb = SAMPLE_KEY_BLOCK
    assert past % kb == 0
    n_blocks = past // kb
    nr = tn * N_HEADS
    bias_rows = jnp.tile(bias, tn).reshape(nr, 1)
    per_b = lambda b, s: (b, 0, 0)
    const2 = lambda b, s: (0, 0)
    blk = pl.BlockSpec((1, kb, hd), lambda b, s: (b, n_blocks - jnp.maximum(s, 1), 0))
    return pl.pallas_call(
        _attn_sample_kernel,
        out_shape=jax.ShapeDtypeStruct((bd, tn, hd), BF16),
        grid=(bd, n_blocks + 1),
        in_specs=[pl.BlockSpec((1, tn, hd), per_b),
                  pl.BlockSpec((nr, 1), const2),
                  pl.BlockSpec((1, k_new.shape[1], hd), per_b),
                  pl.BlockSpec((1, v_new.shape[1], hd), per_b),
                  blk, blk,
                  pl.BlockSpec((PAGE_SIZE, PAGE_SIZE), const2),
                  pl.BlockSpec((kb, kb), const2)],
        out_specs=pl.BlockSpec((1, tn, hd), per_b),
        scratch_shapes=[pltpu.VMEM((nr, hd), BF16), pltpu.VMEM((nr, LANES), F32),
                        pltpu.VMEM((nr, hd), F32)],
        compiler_params=_params("parallel", "arbitrary"),
        name="attn_sample",
    )(q, bias_rows, k_new, v_new, k_past, v_past, _suffix_ones(PAGE_SIZE), _suffix_ones(kb))


def kernel(x_prompt, x_sample, state_conv, cache_k, cache_v, page_table, norm_mix, norm_ffn, conv_w_pw1, conv_b_pw1, conv_w_dw, conv_b_dw, conv_ln_g, conv_ln_b, conv_w_pw2, conv_b_pw2, kv_norm, w_kv, w_q, w_o, b_logit, ffn_w_gate, ffn_w_up, ffn_w_down, final_norm):
    b, t, d = x_prompt.shape
    bd, tn, _ = x_sample.shape
    hd = N_HEADS * HEAD_DIM
    depth = norm_mix.shape[0]
    n_conv = conv_w_pw1.shape[0]
    keep = CONV_WIDTH - 1

    to_time_major = lambda a: a.transpose(1, 0, 2)
    xp = x_prompt.reshape(b * t, d)
    xs = to_time_major(x_sample).reshape(tn * bd, d)
    k_past, v_past = _gather_pages(cache_k, cache_v, page_table)

    conv_state_p, conv_state_s = [], []
    kv_p = kv_s = None
    y_p = y_s = None
    ffn_w_gate, ffn_w_up, ffn_w_down = (w.astype(BF16) for w in (ffn_w_gate, ffn_w_up, ffn_w_down))
    for l in range(depth):
        last = final_norm if l == depth - 1 else None
        ffn_w = (norm_ffn[l], ffn_w_gate[l], ffn_w_up[l], ffn_w_down[l])
        if l < n_conv:
            w1 = conv_w_pw1[l].astype(BF16)
            tail = (conv_w_dw[l], conv_b_dw[l], conv_ln_g[l], conv_ln_b[l],
                    conv_w_pw2[l].astype(BF16), conv_b_pw2[l])
            u_p = _pw1_glu(xp, norm_mix[l], w1, conv_b_pw1[l]).reshape(b, t, d)
            u_s = _pw1_glu(xs, norm_mix[l], w1, conv_b_pw1[l]).reshape(tn, bd, d)
            conv_state_p.append(u_p[:, t - keep:])
            full_s = jnp.concatenate([to_time_major(state_conv[l]), u_s], axis=0)
            conv_state_s.append(to_time_major(full_s[tn:]))
            xp = _conv_prompt(u_p, xp.reshape(b, t, d), *tail).reshape(b * t, d)
            xs = _conv_sample(full_s, xs.reshape(tn, bd, d), *tail).reshape(tn * bd, d)
            xp = _ffn(xp, *ffn_w, final_gain=last)
            xs = _ffn(xs, *ffn_w, final_gain=last)
        else:
            j = l - n_conv
            if kv_p is None:
                wkv = w_kv.astype(BF16)
                kv_p = _kv_proj(xp, kv_norm, wkv)
                kv_s = _kv_proj(xs, kv_norm, wkv)
                new_rows = [jnp.pad(to_time_major(a.reshape(tn, bd, hd)), ((0, 0), (0, 8 - tn), (0, 0)))
                            for a in kv_s[:2]]
            wq = w_q[j].astype(BF16)
            wo = w_o[j].astype(BF16)
            q_p = _q_proj(xp, norm_mix[l], wq).reshape(b, t, hd)
            q_s = to_time_major(_q_proj(xs, norm_mix[l], wq).reshape(tn, bd, hd))
            bias2 = b_logit[j] * LOG2_E
            a_p = _attn_prompt(q_p, kv_p[2].reshape(b, t, hd), kv_p[3].reshape(b, t, hd), bias2)
            a_s = _attn_sample(q_s, new_rows[0], new_rows[1], k_past, v_past, bias2)
            a_s = to_time_major(a_s).reshape(tn * bd, hd)
            xp = _ffn(xp, *ffn_w, attn=a_p.reshape(b * t, hd), wo=wo, final_gain=last)
            xs = _ffn(xs, *ffn_w, attn=a_s, wo=wo, final_gain=last)

    y_p = xp.reshape(b, t, d)
    y_s = to_time_major(xs.reshape(tn, bd, d))
    heads = lambda a, n0, n1: a.reshape(n0, n1, N_HEADS, HEAD_DIM)
    k_s = to_time_major(kv_s[0].reshape(tn, bd, hd))
    v_s = to_time_major(kv_s[1].reshape(tn, bd, hd))
    return (y_p, y_s, jnp.stack(conv_state_p, axis=0), jnp.stack(conv_state_s, axis=0),
            heads(kv_p[0], b, t), heads(kv_p[1], b, t), heads(k_s, bd, tn), heads(v_s, bd, tn))
```

```python
import functools
import math

import jax
import jax.numpy as jnp
from jax import lax
from jax.experimental import pallas as pl
from jax.experimental.pallas import tpu as pltpu

F32 = jnp.float32
BF16 = jnp.bfloat16

N_HEADS = 16
HEAD_DIM = 64
CONV_WIDTH = 31
PAGE_SIZE = 128
RMS_EPS = 1e-6
LN_EPS = 1e-5
LOG2_E = math.log2(math.e)

LANES = 128
SUBLANES = 8
HEADS_PER_LANE_BLOCK = LANES // HEAD_DIM
CONV_HALO = 32
VMEM_LIMIT = 56 * 1024 * 1024

TOKEN_TILE = 512
CONV_TILE = 256
ATTN_Q_BLOCK = 512
ATTN_K_BLOCK = 256
ATTN_UNROLL = 4
BIAS_TERMS = 3
GATHER_PAGE_GROUP = 4
SAMPLE_KEY_BLOCK = 512


def _params(*sem):
    return pltpu.CompilerParams(dimension_semantics=sem, vmem_limit_bytes=VMEM_LIMIT)


def _const_spec(shape):
    nd = len(shape)
    return pl.BlockSpec(shape, lambda *_: (0,) * nd, pipeline_mode=pl.Buffered(1))


def _rms(x, g):
    ms = jnp.mean(x * x, axis=-1, keepdims=True)
    return x * lax.rsqrt(ms + RMS_EPS) * g


def _silu(x):
    return x * jax.nn.sigmoid(x)


def _dot(a, b):
    return jnp.dot(a, b, preferred_element_type=F32)


def _dot_nt(a, b):
    return lax.dot_general(a, b, (((1,), (1,)), ((), ())), preferred_element_type=F32)


def _pw1_glu_kernel(x_ref, g_ref, w_ref, b_ref, u_ref):
    d = u_ref.shape[-1]
    h = _rms(x_ref[...], g_ref[...]).astype(BF16)
    val = _dot(h, w_ref[:, :d]) + b_ref[:, :d]
    gate = _dot(h, w_ref[:, d:]) + b_ref[:, d:]
    u_ref[...] = val * jax.nn.sigmoid(gate)


def _pw1_glu(x, g, w, b):
    n, d = x.shape
    tm = min(TOKEN_TILE, n)
    return pl.pallas_call(
        _pw1_glu_kernel,
        out_shape=jax.ShapeDtypeStruct((n, d), F32),
        grid=(n // tm,),
        in_specs=[pl.BlockSpec((tm, d), lambda i: (i, 0)),
                  _const_spec((1, d)), _const_spec(w.shape), _const_spec((1, 2 * d))],
        out_specs=pl.BlockSpec((tm, d), lambda i: (i, 0)),
        compiler_params=_params("parallel"),
        name="pw1_glu",
    )(x, g.reshape(1, d), w, b.reshape(1, 2 * d))


def _ffn_kernel(*refs, chunks, pre_proj, final_norm):
    refs = list(refs)
    x_ref = refs.pop(0)
    x = x_ref[...]
    if pre_proj:
        a_ref, wo_ref = refs.pop(0), refs.pop(0)
        x = x + _dot(a_ref[...], wo_ref[...])
    g_ref, wg_ref, wu_ref, wd_ref = refs[:4]
    refs = refs[4:]
    h = _rms(x, g_ref[...]).astype(BF16)
    y = x
    for c0, c1 in chunks:
        gate = _dot(h, wg_ref[:, c0:c1])
        up = _dot(h, wu_ref[:, c0:c1])
        act = (_silu(gate) * up).astype(BF16)
        y = y + _dot(act, wd_ref[c0:c1, :])
    if final_norm:
        fg_ref, o_ref = refs
        o_ref[...] = _rms(y, fg_ref[...])
    else:
        (o_ref,) = refs
        o_ref[...] = y


def _ffn(x, g, wg, wu, wd, attn=None, wo=None, final_gain=None):
    n, d = x.shape
    dff = wg.shape[1]
    tm = min(TOKEN_TILE, n)
    step = 2 * LANES * 3
    chunks = tuple((c, min(c + step, dff)) for c in range(0, dff, step))
    row = lambda i: (i, 0)
    args, specs = [x], [pl.BlockSpec((tm, d), row)]
    if attn is not None:
        args += [attn, wo]
        specs += [pl.BlockSpec((tm, attn.shape[1]), row), _const_spec(wo.shape)]
    args += [g.reshape(1, d), wg, wu, wd]
    specs += [_const_spec((1, d)), _const_spec(wg.shape), _const_spec(wu.shape), _const_spec(wd.shape)]
    if final_gain is not None:
        args.append(final_gain.reshape(1, d))
        specs.append(_const_spec((1, d)))
    return pl.pallas_call(
        functools.partial(_ffn_kernel, chunks=chunks, pre_proj=attn is not None,
                          final_norm=final_gain is not None),
        out_shape=jax.ShapeDtypeStruct((n, d), F32),
        grid=(n // tm,),
        in_specs=specs,
        out_specs=pl.BlockSpec((tm, d), row),
        compiler_params=_params("parallel"),
        name="ffn",
    )(*args)


def _kv_proj_kernel(x_ref, g_ref, w_ref, k_ref, v_ref, kb_ref, vb_ref):
    hd = k_ref.shape[-1]
    h = _rms(x_ref[...], g_ref[...]).astype(BF16)
    k = _dot(h, w_ref[:, :hd])
    v = _dot(h, w_ref[:, hd:])
    k_ref[...] = k
    v_ref[...] = v
    kb_ref[...] = k.astype(BF16)
    vb_ref[...] = v.astype(BF16)


def _kv_proj(x, g, w):
    n, d = x.shape
    hd = w.shape[1] // 2
    tm = min(TOKEN_TILE, n)
    row = lambda i: (i, 0)
    out = pl.BlockSpec((tm, hd), row)
    return pl.pallas_call(
        _kv_proj_kernel,
        out_shape=(jax.ShapeDtypeStruct((n, hd), F32), jax.ShapeDtypeStruct((n, hd), F32),
                   jax.ShapeDtypeStruct((n, hd), BF16), jax.ShapeDtypeStruct((n, hd), BF16)),
        grid=(n // tm,),
        in_specs=[pl.BlockSpec((tm, d), row), _const_spec((1, d)), _const_spec(w.shape)],
        out_specs=(out, out, out, out),
        compiler_params=_params("parallel"),
        name="kv_proj",
    )(x, g.reshape(1, d), w)


def _q_proj_kernel(x_ref, g_ref, w_ref, q_ref, *, scale):
    h = _rms(x_ref[...], g_ref[...]).astype(BF16)
    q_ref[...] = (_dot(h, w_ref[...]) * scale).astype(BF16)


def _q_proj(x, g, w):
    n, d = x.shape
    hd = w.shape[1]
    tm = min(TOKEN_TILE, n)
    row = lambda i: (i, 0)
    return pl.pallas_call(
        functools.partial(_q_proj_kernel, scale=LOG2_E / math.sqrt(HEAD_DIM)),
        out_shape=jax.ShapeDtypeStruct((n, hd), BF16),
        grid=(n // tm,),
        in_specs=[pl.BlockSpec((tm, d), row), _const_spec((1, d)), _const_spec(w.shape)],
        out_specs=pl.BlockSpec((tm, hd), row),
        compiler_params=_params("parallel"),
        name="q_proj",
    )(x, g.reshape(1, d), w)


def _ln_swish_pw2(c, x, lng_ref, lnb_ref, w2_ref, b2_ref):
    mu = jnp.mean(c, axis=-1, keepdims=True)
    xc = c - mu
    var = jnp.mean(xc * xc, axis=-1, keepdims=True)
    y = _silu(xc * lax.rsqrt(var + LN_EPS) * lng_ref[...] + lnb_ref[...])
    return x + _dot(y.astype(BF16), w2_ref[...]) + b2_ref[...]


def _conv_prompt_kernel(u_ref, uh_ref, x_ref, wdw_ref, bdw_ref, lng_ref, lnb_ref, w2_ref, b2_ref,
                        o_ref, win_ref, c_ref):
    i = pl.program_id(1)
    tm, d = c_ref.shape
    n = CONV_HALO + tm
    win_ref[0, 0:CONV_HALO, :] = jnp.where(i == 0, 0.0, uh_ref[0])
    win_ref[0, CONV_HALO:, :] = u_ref[0]
    for s in range(1, SUBLANES):
        win_ref[s, 0:n - SUBLANES, :] = win_ref[0, s:s + n - SUBLANES, :]
    first = CONV_HALO - (CONV_WIDTH - 1)
    rows, cols = 32, 256
    for r in range(0, tm, rows):
        for c in range(0, d, cols):
            acc = jnp.broadcast_to(bdw_ref[:, c:c + cols], (rows, cols))
            for w in range(CONV_WIDTH):
                s = (first + w) % SUBLANES
                base = r + first + w - s
                acc = acc + win_ref[s, base:base + rows, c:c + cols] * wdw_ref[w:w + 1, c:c + cols]
            c_ref[r:r + rows, c:c + cols] = acc
    o_ref[0] = _ln_swish_pw2(c_ref[...], x_ref[0], lng_ref, lnb_ref, w2_ref, b2_ref)


def _conv_prompt(u, x, wdw, bdw, lng, lnb, w2, b2):
    b, t, d = u.shape
    tm = CONV_TILE
    per = tm // CONV_HALO
    tile = pl.BlockSpec((1, tm, d), lambda bi, i: (bi, i, 0))
    halo = pl.BlockSpec((1, CONV_HALO, d), lambda bi, i: (bi, jnp.maximum(i * per - 1, 0), 0))
    vec = _const_spec((1, d))
    return pl.pallas_call(
        _conv_prompt_kernel,
        out_shape=jax.ShapeDtypeStruct((b, t, d), F32),
        grid=(b, t // tm),
        in_specs=[tile, halo, tile, _const_spec(wdw.shape), vec, vec, vec, _const_spec(w2.shape), vec],
        out_specs=tile,
        scratch_shapes=[pltpu.VMEM((SUBLANES, CONV_HALO + tm, d), F32), pltpu.VMEM((tm, d), F32)],
        compiler_params=_params("parallel", "parallel"),
        name="conv_prompt",
    )(u, u, x, wdw, bdw.reshape(1, d), lng.reshape(1, d), lnb.reshape(1, d), w2, b2.reshape(1, d))


def _conv_sample_kernel(full_ref, x_ref, wdw_ref, bdw_ref, lng_ref, lnb_ref, w2_ref, b2_ref, o_ref):
    tn, bs, d = x_ref.shape
    for t in range(tn):
        acc = jnp.broadcast_to(bdw_ref[...], (bs, d))
        for w in range(CONV_WIDTH):
            acc = acc + full_ref[t + w] * wdw_ref[w:w + 1, :]
        o_ref[t] = _ln_swish_pw2(acc, x_ref[t], lng_ref, lnb_ref, w2_ref, b2_ref)


def _conv_sample(full, x, wdw, bdw, lng, lnb, w2, b2):
    tn, nb, d = x.shape
    bs = 32
    vec = _const_spec((1, d))
    return pl.pallas_call(
        _conv_sample_kernel,
        out_shape=jax.ShapeDtypeStruct((tn, nb, d), F32),
        grid=(nb // bs,),
        in_specs=[pl.BlockSpec((full.shape[0], bs, d), lambda i: (0, i, 0)),
                  pl.BlockSpec((tn, bs, d), lambda i: (0, i, 0)),
                  _const_spec(wdw.shape), vec, vec, vec, _const_spec(w2.shape), vec],
        out_specs=pl.BlockSpec((tn, bs, d), lambda i: (0, i, 0)),
        compiler_params=_params("parallel"),
        name="conv_sample",
    )(full, x, wdw, bdw.reshape(1, d), lng.reshape(1, d), lnb.reshape(1, d), w2, b2.reshape(1, d))


def _suffix_ones(n):
    j = lax.broadcasted_iota(jnp.int32, (n, n), 0)
    s = lax.broadcasted_iota(jnp.int32, (n, n), 1)
    return (j >= s).astype(BF16)


def _stick_tile(z, carry, u_incl, mask):
    neg_abs = lax.bitcast_convert_type(
        lax.bitcast_convert_type(z, jnp.uint32) | jnp.uint32(0x80000000), F32)
    sp = jnp.maximum(z, 0.0) + jnp.log2(1.0 + jnp.exp2(neg_abs))
    if mask is not None:
        sp = jnp.where(mask, sp, 0.0)
    s_incl = _dot(sp.astype(BF16), u_incl)
    reps = z.shape[1] // LANES
    a = jnp.exp2(z - s_incl - jnp.concatenate([carry] * reps, axis=1))
    if mask is not None:
        a = jnp.where(mask, a, 0.0)
    total = s_incl[:, :1]
    return a.astype(BF16), carry + jnp.broadcast_to(total, carry.shape)


def _attn_prompt_kernel(bias_ref, q_ref, k_ref, v_ref, u_ref, o_ref, qs_ref, carry_ref, acc_ref,
                        *, qb, kb):
    p = pl.program_id(1)
    i = pl.program_id(2)
    nh = HEADS_PER_LANE_BLOCK
    q2 = q_ref[0]
    lane = lax.broadcasted_iota(jnp.int32, (qb, LANES), 1)
    for hh in range(nh):
        in_head = (lane >= hh * HEAD_DIM) & (lane < (hh + 1) * HEAD_DIM)
        qs_ref[hh * qb:(hh + 1) * qb, :LANES] = jnp.where(in_head, q2, jnp.zeros_like(q2))
        rest = jnp.full((qb, LANES), bias_ref[nh * p + hh], F32)
        ext = jnp.zeros((qb, LANES), F32)
        for term in range(BIAS_TERMS):
            part = rest.astype(BF16).astype(F32)
            rest = rest - part
            ext = jnp.where(lane == term, part, ext)
        qs_ref[hh * qb:(hh + 1) * qb, LANES:] = ext.astype(BF16)
    carry_ref[...] = jnp.zeros_like(carry_ref)
    acc_ref[...] = jnp.zeros_like(acc_ref)
    k_lane = lax.broadcasted_iota(jnp.int32, (kb, LANES), 1)
    ones_lanes = jnp.where(k_lane < BIAS_TERMS, 1.0, 0.0).astype(BF16)

    def tile(j, masked):
        start = pl.multiple_of(j * kb, kb)
        k_blk = jnp.concatenate([k_ref[0, pl.ds(start, kb), :], ones_lanes], axis=1)
        v_blk = v_ref[0, pl.ds(start, kb), :]
        z = _dot_nt(qs_ref[...], k_blk)
        mask = None
        if masked:
            row = lax.broadcasted_iota(jnp.int32, (nh * qb, kb), 0)
            q_pos = i * qb + row % qb
            mask = start + lax.broadcasted_iota(jnp.int32, (nh * qb, kb), 1) < q_pos
        a, carry = _stick_tile(z, carry_ref[...], u_ref[...], mask)
        carry_ref[...] = carry
        acc_ref[...] += _dot(a, v_blk)

    n_diag = qb // kb
    first = i * n_diag
    for s in range(n_diag):
        tile(first + n_diag - 1 - s, True)

    assert ATTN_UNROLL == 2 * n_diag
    odd = first % ATTN_UNROLL

    @pl.when(odd > 0)
    def _():
        for s in range(n_diag):
            tile(first - 1 - s, False)

    @pl.loop(0, first // ATTN_UNROLL)
    def _(g):
        for s in range(ATTN_UNROLL):
            tile(first - odd - 1 - g * ATTN_UNROLL - s, False)

    out = jnp.zeros((qb, LANES), F32)
    for hh in range(nh):
        in_head = (lane >= hh * HEAD_DIM) & (lane < (hh + 1) * HEAD_DIM)
        out = jnp.where(in_head, acc_ref[hh * qb:(hh + 1) * qb, :], out)
    o_ref[0] = out.astype(o_ref.dtype)


def _attn_prompt(q, k, v, bias):
    b, t, hd = q.shape
    qb, kb = ATTN_Q_BLOCK, ATTN_K_BLOCK
    rows = HEADS_PER_LANE_BLOCK * qb
    kv_spec = pl.BlockSpec((1, t, LANES), lambda bi, p, i: (bi, 0, p))
    q_spec = pl.BlockSpec((1, qb, LANES), lambda bi, p, i: (bi, i, p))
    return pl.pallas_call(
        functools.partial(_attn_prompt_kernel, qb=qb, kb=kb),
        out_shape=jax.ShapeDtypeStruct((b, t, hd), BF16),
        grid=(b, hd // LANES, t // qb),
        in_specs=[pl.BlockSpec(memory_space=pltpu.SMEM), q_spec, kv_spec, kv_spec,
                  _const_spec((kb, kb))],
        out_specs=q_spec,
        scratch_shapes=[pltpu.VMEM((rows, 2 * LANES), BF16), pltpu.VMEM((rows, LANES), F32),
                        pltpu.VMEM((rows, LANES), F32)],
        compiler_params=_params("parallel", "parallel", "arbitrary"),
        name="attn_prompt",
    )(bias, q, k, v, _suffix_ones(kb))


def _gather_pages_kernel(pt_ref, *refs, group):
    del pt_ref
    ko_ref, vo_ref = refs[2 * group:]
    for srcs, dst in ((refs[:group], ko_ref), (refs[group:2 * group], vo_ref)):
        for r, src in enumerate(srcs):
            flat = src.reshape(PAGE_SIZE * N_HEADS, HEAD_DIM)
            for h in range(N_HEADS):
                rows = flat[pl.ds(h, PAGE_SIZE, stride=N_HEADS), :]
                dst[0, r * PAGE_SIZE:(r + 1) * PAGE_SIZE, h * HEAD_DIM:(h + 1) * HEAD_DIM] = (
                    rows.astype(dst.dtype))


def _gather_pages(cache_k, cache_v, page_table):
    bd, n_pages = page_table.shape
    hd = N_HEADS * HEAD_DIM
    group = GATHER_PAGE_GROUP
    assert n_pages % group == 0
    srcs = [pl.BlockSpec((1, PAGE_SIZE, N_HEADS, HEAD_DIM),
                         lambda b, g, pt, r=r: (pt[b, g * group + r], 0, 0, 0)) for r in range(group)]
    dst = pl.BlockSpec((1, group * PAGE_SIZE, hd), lambda b, g, pt: (b, g, 0))
    out = jax.ShapeDtypeStruct((bd, n_pages * PAGE_SIZE, hd), BF16)
    return pl.pallas_call(
        functools.partial(_gather_pages_kernel, group=group),
        out_shape=(out, out),
        grid_spec=pltpu.PrefetchScalarGridSpec(
            num_scalar_prefetch=1, grid=(bd, n_pages // group), in_specs=srcs + srcs,
            out_specs=(dst, dst)),
        compiler_params=_params("parallel", "parallel"),
        name="gather_pages",
    )(page_table, *([cache_k] * group), *([cache_v] * group))


def _attn_sample_kernel(q_ref, bias_ref, kn_ref, vn_ref, kp_ref, vp_ref, u_new_ref, u_ref, o_ref,
                        qx_ref, carry_ref, acc_ref):
    s = pl.program_id(1)
    tn = q_ref.shape[1]
    nr = tn * N_HEADS
    hd = N_HEADS * HEAD_DIM
    row = lax.broadcasted_iota(jnp.int32, (nr, hd), 0)
    lane_head = lax.broadcasted_iota(jnp.int32, (nr, hd), 1) // HEAD_DIM
    own_head = (row % N_HEADS) == lane_head

    def visit(k_blk, v_blk, u_incl, mask):
        z = _dot_nt(qx_ref[...], k_blk) + bias_ref[...]
        a, carry = _stick_tile(z, carry_ref[...], u_incl, mask)
        carry_ref[...] = carry
        acc_ref[...] += _dot(a, v_blk)

    @pl.when(s == 0)
    def _():
        q4 = q_ref[0].astype(F32)
        rows = jnp.concatenate(
            [jnp.broadcast_to(q4[i:i + 1], (N_HEADS, hd)) for i in range(tn)], axis=0)
        qx_ref[...] = jnp.where(own_head, rows, 0.0).astype(BF16)
        carry_ref[...] = jnp.zeros_like(carry_ref)
        acc_ref[...] = jnp.zeros_like(acc_ref)
        pad = jnp.zeros((PAGE_SIZE - kn_ref.shape[1], hd), F32)
        k_new = jnp.concatenate([kn_ref[0], pad], axis=0).astype(BF16)
        v_new = jnp.concatenate([vn_ref[0], pad], axis=0).astype(BF16)
        q_idx = lax.broadcasted_iota(jnp.int32, (nr, PAGE_SIZE), 0) // N_HEADS
        k_idx = lax.broadcasted_iota(jnp.int32, (nr, PAGE_SIZE), 1)
        visit(k_new, v_new, u_new_ref[...], k_idx < q_idx)

    @pl.when(s > 0)
    def _():
        visit(kp_ref[0], vp_ref[0], u_ref[...], None)

    @pl.when(s == pl.num_programs(1) - 1)
    def _():
        sel = jnp.where(own_head, acc_ref[...], 0.0).reshape(tn, N_HEADS, hd)
        o_ref[0] = jnp.sum(sel, axis=1).astype(o_ref.dtype)


def _attn_sample(q, k_new, v_new, k_past, v_past, bias):
    bd, tn, hd = q.shape
    past = k_past.shape[1]
    kb = SAMPLE_KEY_BLOCK
    assert past % kb == 0
    n_blocks = past // kb
    nr = tn * N_HEADS
    bias_rows = jnp.tile(bias, tn).reshape(nr, 1)
    per_b = lambda b, s: (b, 0, 0)
    const2 = lambda b, s: (0, 0)
    blk = pl.BlockSpec((1, kb, hd), lambda b, s: (b, n_blocks - jnp.maximum(s, 1), 0))
    return pl.pallas_call(
        _attn_sample_kernel,
        out_shape=jax.ShapeDtypeStruct((bd, tn, hd), BF16),
        grid=(bd, n_blocks + 1),
        in_specs=[pl.BlockSpec((1, tn, hd), per_b),
                  pl.BlockSpec((nr, 1), const2),
                  pl.BlockSpec((1, k_new.shape[1], hd), per_b),
                  pl.BlockSpec((1, v_new.shape[1], hd), per_b),
                  blk, blk,
                  pl.BlockSpec((PAGE_SIZE, PAGE_SIZE), const2),
                  pl.BlockSpec((kb, kb), const2)],
        out_specs=pl.BlockSpec((1, tn, hd), per_b),
        scratch_shapes=[pltpu.VMEM((nr, hd), BF16), pltpu.VMEM((nr, LANES), F32),
                        pltpu.VMEM((nr, hd), F32)],
        compiler_params=_params("parallel", "arbitrary"),
        name="attn_sample",
    )(q, bias_rows, k_new, v_new, k_past, v_past, _suffix_ones(PAGE_SIZE), _suffix_ones(kb))


def kernel(x_prompt, x_sample, state_conv, cache_k, cache_v, page_table, norm_mix, norm_ffn, conv_w_pw1, conv_b_pw1, conv_w_dw, conv_b_dw, conv_ln_g, conv_ln_b, conv_w_pw2, conv_b_pw2, kv_norm, w_kv, w_q, w_o, b_logit, ffn_w_gate, ffn_w_up, ffn_w_down, final_norm):
    b, t, d = x_prompt.shape
    bd, tn, _ = x_sample.shape
    hd = N_HEADS * HEAD_DIM
    depth = norm_mix.shape[0]
    n_conv = conv_w_pw1.shape[0]
    keep = CONV_WIDTH - 1

    to_time_major = lambda a: a.transpose(1, 0, 2)
    xp = x_prompt.reshape(b * t, d)
    xs = to_time_major(x_sample).reshape(tn * bd, d)
    k_past, v_past = _gather_pages(cache_k, cache_v, page_table)

    conv_state_p, conv_state_s = [], []
    kv_p = kv_s = None
    y_p = y_s = None
    ffn_w_gate, ffn_w_up, ffn_w_down = (w.astype(BF16) for w in (ffn_w_gate, ffn_w_up, ffn_w_down))
    for l in range(depth):
        last = final_norm if l == depth - 1 else None
        ffn_w = (norm_ffn[l], ffn_w_gate[l], ffn_w_up[l], ffn_w_down[l])
        if l < n_conv:
            w1 = conv_w_pw1[l].astype(BF16)
            tail = (conv_w_dw[l], conv_b_dw[l], conv_ln_g[l], conv_ln_b[l],
                    conv_w_pw2[l].astype(BF16), conv_b_pw2[l])
            u_p = _pw1_glu(xp, norm_mix[l], w1, conv_b_pw1[l]).reshape(b, t, d)
            u_s = _pw1_glu(xs, norm_mix[l], w1, conv_b_pw1[l]).reshape(tn, bd, d)
            conv_state_p.append(u_p[:, t - keep:])
            full_s = jnp.concatenate([to_time_major(state_conv[l]), u_s], axis=0)
            conv_state_s.append(to_time_major(full_s[tn:]))
            xp = _conv_prompt(u_p, xp.reshape(b, t, d), *tail).reshape(b * t, d)
            xs = _conv_sample(full_s, xs.reshape(tn, bd, d), *tail).reshape(tn * bd, d)
            xp = _ffn(xp, *ffn_w, final_gain=last)
            xs = _ffn(xs, *ffn_w, final_gain=last)
        else:
            j = l - n_conv
            if kv_p is None:
                wkv = w_kv.astype(BF16)
                kv_p = _kv_proj(xp, kv_norm, wkv)
                kv_s = _kv_proj(xs, kv_norm, wkv)
                new_rows = [jnp.pad(to_time_major(a.reshape(tn, bd, hd)), ((0, 0), (0, 8 - tn), (0, 0)))
                            for a in kv_s[:2]]
            wq = w_q[j].astype(BF16)
            wo = w_o[j].astype(BF16)
            q_p = _q_proj(xp, norm_mix[l], wq).reshape(b, t, hd)
            q_s = to_time_major(_q_proj(xs, norm_mix[l], wq).reshape(tn, bd, hd))
            bias2 = b_logit[j] * LOG2_E
            a_p = _attn_prompt(q_p, kv_p[2].reshape(b, t, hd), kv_p[3].reshape(b, t, hd), bias2)
            a_s = _attn_sample(q_s, new_rows[0], new_rows[1], k_past, v_past, bias2)
            a_s = to_time_major(a_s).reshape(tn * bd, hd)
            xp = _ffn(xp, *ffn_w, attn=a_p.reshape(b * t, hd), wo=wo, final_gain=last)
            xs = _ffn(xs, *ffn_w, attn=a_s, wo=wo, final_gain=last)

    y_p = xp.reshape(b, t, d)
    y_s = to_time_major(xs.reshape(tn, bd, d))
    heads = lambda a, n0, n1: a.reshape(n0, n1, N_HEADS, HEAD_DIM)
    k_s = to_time_major(kv_s[0].reshape(tn, bd, hd))
    v_s = to_time_major(kv_s[1].reshape(tn, bd, hd))
    return (y_p, y_s, jnp.stack(conv_state_p, axis=0), jnp.stack(conv_state_s, axis=0),
            heads(kv_p[0], b, t), heads(kv_p[1], b, t), heads(k_s, bd, tn), heads(v_s, bd, tn))
```

```python
import functools
import math

import jax
import jax.numpy as jnp
from jax import lax
from jax.experimental import pallas as pl
from jax.experimental.pallas import tpu as pltpu

F32 = jnp.float32
BF16 = jnp.bfloat16

N_HEADS = 16
HEAD_DIM = 64
CONV_WIDTH = 31
PAGE_SIZE = 128
RMS_EPS = 1e-6
LN_EPS = 1e-5
LOG2_E = math.log2(math.e)

LANES = 128
SUBLANES = 8
HEADS_PER_LANE_BLOCK = LANES // HEAD_DIM
CONV_HALO = 32
VMEM_LIMIT = 56 * 1024 * 1024

TOKEN_TILE = 512
CONV_TILE = 256
ATTN_Q_BLOCK = 512
ATTN_K_BLOCK = 256
ATTN_UNROLL = 4
BIAS_TERMS = 3
SAMPLE_PAGE_GROUP = 4


def _params(*sem):
    return pltpu.CompilerParams(dimension_semantics=sem, vmem_limit_bytes=VMEM_LIMIT)


def _const_spec(shape):
    nd = len(shape)
    return pl.BlockSpec(shape, lambda *_: (0,) * nd, pipeline_mode=pl.Buffered(1))


def _rms(x, g):
    ms = jnp.mean(x * x, axis=-1, keepdims=True)
    return x * lax.rsqrt(ms + RMS_EPS) * g


def _silu(x):
    return x * jax.nn.sigmoid(x)


def _dot(a, b):
    return jnp.dot(a, b, preferred_element_type=F32)


def _dot_nt(a, b):
    return lax.dot_general(a, b, (((1,), (1,)), ((), ())), preferred_element_type=F32)


def _pw1_glu_kernel(x_ref, g_ref, w_ref, b_ref, u_ref):
    d = u_ref.shape[-1]
    h = _rms(x_ref[...], g_ref[...]).astype(BF16)
    val = _dot(h, w_ref[:, :d]) + b_ref[:, :d]
    gate = _dot(h, w_ref[:, d:]) + b_ref[:, d:]
    u_ref[...] = val * jax.nn.sigmoid(gate)


def _pw1_glu(x, g, w, b):
    n, d = x.shape
    tm = min(TOKEN_TILE, n)
    return pl.pallas_call(
        _pw1_glu_kernel,
        out_shape=jax.ShapeDtypeStruct((n, d), F32),
        grid=(n // tm,),
        in_specs=[pl.BlockSpec((tm, d), lambda i: (i, 0)),
                  _const_spec((1, d)), _const_spec(w.shape), _const_spec((1, 2 * d))],
        out_specs=pl.BlockSpec((tm, d), lambda i: (i, 0)),
        compiler_params=_params("parallel"),
        name="pw1_glu",
    )(x, g.reshape(1, d), w, b.reshape(1, 2 * d))


def _ffn_kernel(*refs, chunks, pre_proj, final_norm):
    refs = list(refs)
    x_ref = refs.pop(0)
    x = x_ref[...]
    if pre_proj:
        a_ref, wo_ref = refs.pop(0), refs.pop(0)
        x = x + _dot(a_ref[...], wo_ref[...])
    g_ref, wg_ref, wu_ref, wd_ref = refs[:4]
    refs = refs[4:]
    h = _rms(x, g_ref[...]).astype(BF16)
    y = x
    for c0, c1 in chunks:
        gate = _dot(h, wg_ref[:, c0:c1])
        up = _dot(h, wu_ref[:, c0:c1])
        act = (_silu(gate) * up).astype(BF16)
        y = y + _dot(act, wd_ref[c0:c1, :])
    if final_norm:
        fg_ref, o_ref = refs
        o_ref[...] = _rms(y, fg_ref[...])
    else:
        (o_ref,) = refs
        o_ref[...] = y


def _ffn(x, g, wg, wu, wd, attn=None, wo=None, final_gain=None):
    n, d = x.shape
    dff = wg.shape[1]
    tm = min(TOKEN_TILE, n)
    step = 2 * LANES * 3
    chunks = tuple((c, min(c + step, dff)) for c in range(0, dff, step))
    row = lambda i: (i, 0)
    args, specs = [x], [pl.BlockSpec((tm, d), row)]
    if attn is not None:
        args += [attn, wo]
        specs += [pl.BlockSpec((tm, attn.shape[1]), row), _const_spec(wo.shape)]
    args += [g.reshape(1, d), wg, wu, wd]
    specs += [_const_spec((1, d)), _const_spec(wg.shape), _const_spec(wu.shape), _const_spec(wd.shape)]
    if final_gain is not None:
        args.append(final_gain.reshape(1, d))
        specs.append(_const_spec((1, d)))
    return pl.pallas_call(
        functools.partial(_ffn_kernel, chunks=chunks, pre_proj=attn is not None,
                          final_norm=final_gain is not None),
        out_shape=jax.ShapeDtypeStruct((n, d), F32),
        grid=(n // tm,),
        in_specs=specs,
        out_specs=pl.BlockSpec((tm, d), row),
        compiler_params=_params("parallel"),
        name="ffn",
    )(*args)


def _kv_proj_kernel(x_ref, g_ref, w_ref, k_ref, v_ref, kb_ref, vb_ref):
    hd = k_ref.shape[-1]
    h = _rms(x_ref[...], g_ref[...]).astype(BF16)
    k = _dot(h, w_ref[:, :hd])
    v = _dot(h, w_ref[:, hd:])
    k_ref[...] = k
    v_ref[...] = v
    kb_ref[...] = k.astype(BF16)
    vb_ref[...] = v.astype(BF16)


def _kv_proj(x, g, w):
    n, d = x.shape
    hd = w.shape[1] // 2
    tm = min(TOKEN_TILE, n)
    row = lambda i: (i, 0)
    out = pl.BlockSpec((tm, hd), row)
    return pl.pallas_call(
        _kv_proj_kernel,
        out_shape=(jax.ShapeDtypeStruct((n, hd), F32), jax.ShapeDtypeStruct((n, hd), F32),
                   jax.ShapeDtypeStruct((n, hd), BF16), jax.ShapeDtypeStruct((n, hd), BF16)),
        grid=(n // tm,),
        in_specs=[pl.BlockSpec((tm, d), row), _const_spec((1, d)), _const_spec(w.shape)],
        out_specs=(out, out, out, out),
        compiler_params=_params("parallel"),
        name="kv_proj",
    )(x, g.reshape(1, d), w)


def _q_proj_kernel(x_ref, g_ref, w_ref, q_ref, *, scale):
    h = _rms(x_ref[...], g_ref[...]).astype(BF16)
    q_ref[...] = (_dot(h, w_ref[...]) * scale).astype(BF16)


def _q_proj(x, g, w):
    n, d = x.shape
    hd = w.shape[1]
    tm = min(TOKEN_TILE, n)
    row = lambda i: (i, 0)
    return pl.pallas_call(
        functools.partial(_q_proj_kernel, scale=LOG2_E / math.sqrt(HEAD_DIM)),
        out_shape=jax.ShapeDtypeStruct((n, hd), BF16),
        grid=(n // tm,),
        in_specs=[pl.BlockSpec((tm, d), row), _const_spec((1, d)), _const_spec(w.shape)],
        out_specs=pl.BlockSpec((tm, hd), row),
        compiler_params=_params("parallel"),
        name="q_proj",
    )(x, g.reshape(1, d), w)


def _ln_swish_pw2(c, x, lng_ref, lnb_ref, w2_ref, b2_ref):
    mu = jnp.mean(c, axis=-1, keepdims=True)
    xc = c - mu
    var = jnp.mean(xc * xc, axis=-1, keepdims=True)
    y = _silu(xc * lax.rsqrt(var + LN_EPS) * lng_ref[...] + lnb_ref[...])
    return x + _dot(y.astype(BF16), w2_ref[...]) + b2_ref[...]


def _conv_prompt_kernel(u_ref, uh_ref, x_ref, wdw_ref, bdw_ref, lng_ref, lnb_ref, w2_ref, b2_ref,
                        o_ref, win_ref, c_ref):
    i = pl.program_id(1)
    tm, d = c_ref.shape
    n = CONV_HALO + tm
    win_ref[0, 0:CONV_HALO, :] = jnp.where(i == 0, 0.0, uh_ref[0])
    win_ref[0, CONV_HALO:, :] = u_ref[0]
    for s in range(1, SUBLANES):
        win_ref[s, 0:n - SUBLANES, :] = win_ref[0, s:s + n - SUBLANES, :]
    first = CONV_HALO - (CONV_WIDTH - 1)
    rows, cols = 32, 256
    for r in range(0, tm, rows):
        for c in range(0, d, cols):
            acc = jnp.broadcast_to(bdw_ref[:, c:c + cols], (rows, cols))
            for w in range(CONV_WIDTH):
                s = (first + w) % SUBLANES
                base = r + first + w - s
                acc = acc + win_ref[s, base:base + rows, c:c + cols] * wdw_ref[w:w + 1, c:c + cols]
            c_ref[r:r + rows, c:c + cols] = acc
    o_ref[0] = _ln_swish_pw2(c_ref[...], x_ref[0], lng_ref, lnb_ref, w2_ref, b2_ref)


def _conv_prompt(u, x, wdw, bdw, lng, lnb, w2, b2):
    b, t, d = u.shape
    tm = CONV_TILE
    per = tm // CONV_HALO
    tile = pl.BlockSpec((1, tm, d), lambda bi, i: (bi, i, 0))
    halo = pl.BlockSpec((1, CONV_HALO, d), lambda bi, i: (bi, jnp.maximum(i * per - 1, 0), 0))
    vec = _const_spec((1, d))
    return pl.pallas_call(
        _conv_prompt_kernel,
        out_shape=jax.ShapeDtypeStruct((b, t, d), F32),
        grid=(b, t // tm),
        in_specs=[tile, halo, tile, _const_spec(wdw.shape), vec, vec, vec, _const_spec(w2.shape), vec],
        out_specs=tile,
        scratch_shapes=[pltpu.VMEM((SUBLANES, CONV_HALO + tm, d), F32), pltpu.VMEM((tm, d), F32)],
        compiler_params=_params("parallel", "parallel"),
        name="conv_prompt",
    )(u, u, x, wdw, bdw.reshape(1, d), lng.reshape(1, d), lnb.reshape(1, d), w2, b2.reshape(1, d))


def _conv_sample_kernel(full_ref, x_ref, wdw_ref, bdw_ref, lng_ref, lnb_ref, w2_ref, b2_ref, o_ref):
    tn, bs, d = x_ref.shape
    for t in range(tn):
        acc = jnp.broadcast_to(bdw_ref[...], (bs, d))
        for w in range(CONV_WIDTH):
            acc = acc + full_ref[t + w] * wdw_ref[w:w + 1, :]
        o_ref[t] = _ln_swish_pw2(acc, x_ref[t], lng_ref, lnb_ref, w2_ref, b2_ref)


def _conv_sample(full, x, wdw, bdw, lng, lnb, w2, b2):
    tn, nb, d = x.shape
    bs = 32
    vec = _const_spec((1, d))
    return pl.pallas_call(
        _conv_sample_kernel,
        out_shape=jax.ShapeDtypeStruct((tn, nb, d), F32),
        grid=(nb // bs,),
        in_specs=[pl.BlockSpec((full.shape[0], bs, d), lambda i: (0, i, 0)),
                  pl.BlockSpec((tn, bs, d), lambda i: (0, i, 0)),
                  _const_spec(wdw.shape), vec, vec, vec, _const_spec(w2.shape), vec],
        out_specs=pl.BlockSpec((tn, bs, d), lambda i: (0, i, 0)),
        compiler_params=_params("parallel"),
        name="conv_sample",
    )(full, x, wdw, bdw.reshape(1, d), lng.reshape(1, d), lnb.reshape(1, d), w2, b2.reshape(1, d))


def _suffix_ones(n):
    j = lax.broadcasted_iota(jnp.int32, (n, n), 0)
    s = lax.broadcasted_iota(jnp.int32, (n, n), 1)
    return (j >= s).astype(BF16)


def _stick_tile(z, carry, u_incl, mask):
    neg_abs = lax.bitcast_convert_type(
        lax.bitcast_convert_type(z, jnp.uint32) | jnp.uint32(0x80000000), F32)
    sp = jnp.maximum(z, 0.0) + jnp.log2(1.0 + jnp.exp2(neg_abs))
    if mask is not None:
        sp = jnp.where(mask, sp, 0.0)
    s_incl = _dot(sp.astype(BF16), u_incl)
    reps = z.shape[1] // LANES
    a = jnp.exp2(z - s_incl - jnp.concatenate([carry] * reps, axis=1))
    if mask is not None:
        a = jnp.where(mask, a, 0.0)
    total = s_incl[:, :1]
    return a.astype(BF16), carry + jnp.broadcast_to(total, carry.shape)


def _attn_prompt_kernel(bias_ref, q_ref, k_ref, v_ref, u_ref, o_ref, qs_ref, carry_ref, acc_ref,
                        *, qb, kb):
    p = pl.program_id(1)
    i = pl.program_id(2)
    nh = HEADS_PER_LANE_BLOCK
    q2 = q_ref[0]
    lane = lax.broadcasted_iota(jnp.int32, (qb, LANES), 1)
    for hh in range(nh):
        in_head = (lane >= hh * HEAD_DIM) & (lane < (hh + 1) * HEAD_DIM)
        qs_ref[hh * qb:(hh + 1) * qb, :LANES] = jnp.where(in_head, q2, jnp.zeros_like(q2))
        rest = jnp.full((qb, LANES), bias_ref[nh * p + hh], F32)
        ext = jnp.zeros((qb, LANES), F32)
        for term in range(BIAS_TERMS):
            part = rest.astype(BF16).astype(F32)
            rest = rest - part
            ext = jnp.where(lane == term, part, ext)
        qs_ref[hh * qb:(hh + 1) * qb, LANES:] = ext.astype(BF16)
    carry_ref[...] = jnp.zeros_like(carry_ref)
    acc_ref[...] = jnp.zeros_like(acc_ref)
    k_lane = lax.broadcasted_iota(jnp.int32, (kb, LANES), 1)
    ones_lanes = jnp.where(k_lane < BIAS_TERMS, 1.0, 0.0).astype(BF16)

    def tile(j, masked):
        start = pl.multiple_of(j * kb, kb)
        k_blk = jnp.concatenate([k_ref[0, pl.ds(start, kb), :], ones_lanes], axis=1)
        v_blk = v_ref[0, pl.ds(start, kb), :]
        z = _dot_nt(qs_ref[...], k_blk)
        mask = None
        if masked:
            row = lax.broadcasted_iota(jnp.int32, (nh * qb, kb), 0)
            q_pos = i * qb + row % qb
            mask = start + lax.broadcasted_iota(jnp.int32, (nh * qb, kb), 1) < q_pos
        a, carry = _stick_tile(z, carry_ref[...], u_ref[...], mask)
        carry_ref[...] = carry
        acc_ref[...] += _dot(a, v_blk)

    n_diag = qb // kb
    first = i * n_diag
    for s in range(n_diag):
        tile(first + n_diag - 1 - s, True)

    assert ATTN_UNROLL == 2 * n_diag
    odd = first % ATTN_UNROLL

    @pl.when(odd > 0)
    def _():
        for s in range(n_diag):
            tile(first - 1 - s, False)

    @pl.loop(0, first // ATTN_UNROLL)
    def _(g):
        for s in range(ATTN_UNROLL):
            tile(first - odd - 1 - g * ATTN_UNROLL - s, False)

    out = jnp.zeros((qb, LANES), F32)
    for hh in range(nh):
        in_head = (lane >= hh * HEAD_DIM) & (lane < (hh + 1) * HEAD_DIM)
        out = jnp.where(in_head, acc_ref[hh * qb:(hh + 1) * qb, :], out)
    o_ref[0] = out.astype(o_ref.dtype)


def _attn_prompt(q, k, v, bias):
    b, t, hd = q.shape
    qb, kb = ATTN_Q_BLOCK, ATTN_K_BLOCK
    rows = HEADS_PER_LANE_BLOCK * qb
    kv_spec = pl.BlockSpec((1, t, LANES), lambda bi, p, i: (bi, 0, p))
    q_spec = pl.BlockSpec((1, qb, LANES), lambda bi, p, i: (bi, i, p))
    return pl.pallas_call(
        functools.partial(_attn_prompt_kernel, qb=qb, kb=kb),
        out_shape=jax.ShapeDtypeStruct((b, t, hd), BF16),
        grid=(b, hd // LANES, t // qb),
        in_specs=[pl.BlockSpec(memory_space=pltpu.SMEM), q_spec, kv_spec, kv_spec,
                  _const_spec((kb, kb))],
        out_specs=q_spec,
        scratch_shapes=[pltpu.VMEM((rows, 2 * LANES), BF16), pltpu.VMEM((rows, LANES), F32),
                        pltpu.VMEM((rows, LANES), F32)],
        compiler_params=_params("parallel", "parallel", "arbitrary"),
        name="attn_prompt",
    )(bias, q, k, v, _suffix_ones(kb))


def _attn_sample_kernel(pt_ref, q_ref, bias_ref, kn_ref, vn_ref, *refs, group):
    del pt_ref
    kp_refs, vp_refs = refs[:group], refs[group:2 * group]
    uu_new_ref, uu_ref, o_ref, qx_ref, carry_ref, acc_ref = refs[2 * group:]
    s = pl.program_id(1)
    tn = q_ref.shape[1]
    nr = tn * N_HEADS
    hd = N_HEADS * HEAD_DIM
    row = lax.broadcasted_iota(jnp.int32, (nr, hd), 0)
    lane_head = lax.broadcasted_iota(jnp.int32, (nr, hd), 1) // HEAD_DIM
    own_head = (row % N_HEADS) == lane_head

    def visit(k_pages, v_pages, uu, mask):
        z = jnp.concatenate([_dot_nt(qx_ref[...], kp) for kp in k_pages], axis=1) + bias_ref[...]
        a, carry = _stick_tile(z, carry_ref[...], uu, mask)
        carry_ref[...] = carry
        acc = acc_ref[...]
        for r, vp in enumerate(v_pages):
            acc = acc + _dot(a[:, r * PAGE_SIZE:(r + 1) * PAGE_SIZE], vp)
        acc_ref[...] = acc

    @pl.when(s == 0)
    def _():
        q4 = q_ref[0].astype(F32)
        rows = jnp.concatenate(
            [jnp.broadcast_to(q4[i:i + 1], (N_HEADS, hd)) for i in range(tn)], axis=0)
        qx_ref[...] = jnp.where(own_head, rows, 0.0).astype(BF16)
        carry_ref[...] = jnp.zeros_like(carry_ref)
        acc_ref[...] = jnp.zeros_like(acc_ref)
        pad = jnp.zeros((PAGE_SIZE - kn_ref.shape[1], hd), F32)
        k_new = jnp.concatenate([kn_ref[0], pad], axis=0).astype(BF16)
        v_new = jnp.concatenate([vn_ref[0], pad], axis=0).astype(BF16)
        q_idx = lax.broadcasted_iota(jnp.int32, (nr, PAGE_SIZE), 0) // N_HEADS
        k_idx = lax.broadcasted_iota(jnp.int32, (nr, PAGE_SIZE), 1)
        visit([k_new], [v_new], uu_new_ref[...], k_idx < q_idx)

    @pl.when(s > 0)
    def _():
        visit([r[0].astype(BF16) for r in kp_refs], [r[0].astype(BF16) for r in vp_refs],
              uu_ref[...], None)

    @pl.when(s == pl.num_programs(1) - 1)
    def _():
        sel = jnp.where(own_head, acc_ref[...], 0.0).reshape(tn, N_HEADS, hd)
        o_ref[0] = jnp.sum(sel, axis=1).astype(o_ref.dtype)


def _attn_sample(q, k_new, v_new, cache_k, cache_v, page_table, bias):
    bd, tn, hd = q.shape
    n_pages = page_table.shape[1]
    group = SAMPLE_PAGE_GROUP
    assert n_pages % group == 0
    nr = tn * N_HEADS
    bias_rows = jnp.tile(bias, tn).reshape(nr, 1)

    def page(r):
        return lambda b, s, pt: (pt[b, n_pages - group * jnp.maximum(s, 1) + r], 0, 0)

    per_b = lambda b, s, pt: (b, 0, 0)
    const2 = lambda b, s, pt: (0, 0)
    page_specs = [pl.BlockSpec((1, PAGE_SIZE, hd), page(r)) for r in range(group)]
    grid_spec = pltpu.PrefetchScalarGridSpec(
        num_scalar_prefetch=1,
        grid=(bd, n_pages // group + 1),
        in_specs=[pl.BlockSpec((1, tn, hd), per_b),
                  pl.BlockSpec((nr, 1), const2),
                  pl.BlockSpec((1, k_new.shape[1], hd), per_b),
                  pl.BlockSpec((1, v_new.shape[1], hd), per_b)]
                 + page_specs + page_specs
                 + [pl.BlockSpec((PAGE_SIZE, PAGE_SIZE), const2),
                    pl.BlockSpec((group * PAGE_SIZE, group * PAGE_SIZE), const2)],
        out_specs=pl.BlockSpec((1, tn, hd), per_b),
        scratch_shapes=[pltpu.VMEM((nr, hd), BF16), pltpu.VMEM((nr, LANES), F32),
                        pltpu.VMEM((nr, hd), F32)],
    )
    return pl.pallas_call(
        functools.partial(_attn_sample_kernel, group=group),
        out_shape=jax.ShapeDtypeStruct((bd, tn, hd), BF16),
        grid_spec=grid_spec,
        compiler_params=_params("parallel", "arbitrary"),
        name="attn_sample",
    )(page_table, q, bias_rows, k_new, v_new, *([cache_k] * group), *([cache_v] * group),
      _suffix_ones(PAGE_SIZE), _suffix_ones(group * PAGE_SIZE))


def kernel(x_prompt, x_sample, state_conv, cache_k, cache_v, page_table, norm_mix, norm_ffn, conv_w_pw1, conv_b_pw1, conv_w_dw, conv_b_dw, conv_ln_g, conv_ln_b, conv_w_pw2, conv_b_pw2, kv_norm, w_kv, w_q, w_o, b_logit, ffn_w_gate, ffn_w_up, ffn_w_down, final_norm):
    b, t, d = x_prompt.shape
    bd, tn, _ = x_sample.shape
    hd = N_HEADS * HEAD_DIM
    depth = norm_mix.shape[0]
    n_conv = conv_w_pw1.shape[0]
    n_phys = cache_k.shape[0]
    keep = CONV_WIDTH - 1

    to_time_major = lambda a: a.transpose(1, 0, 2)
    xp = x_prompt.reshape(b * t, d)
    xs = to_time_major(x_sample).reshape(tn * bd, d)
    cache_k = cache_k.reshape(n_phys, PAGE_SIZE, hd)
    cache_v = cache_v.reshape(n_phys, PAGE_SIZE, hd)

    conv_state_p, conv_state_s = [], []
    kv_p = kv_s = None
    y_p = y_s = None
    for l in range(depth):
        last = final_norm if l == depth - 1 else None
        ffn_w = (norm_ffn[l], ffn_w_gate[l].astype(BF16), ffn_w_up[l].astype(BF16),
                 ffn_w_down[l].astype(BF16))
        if l < n_conv:
            w1 = conv_w_pw1[l].astype(BF16)
            tail = (conv_w_dw[l], conv_b_dw[l], conv_ln_g[l], conv_ln_b[l],
                    conv_w_pw2[l].astype(BF16), conv_b_pw2[l])
            u_p = _pw1_glu(xp, norm_mix[l], w1, conv_b_pw1[l]).reshape(b, t, d)
            u_s = _pw1_glu(xs, norm_mix[l], w1, conv_b_pw1[l]).reshape(tn, bd, d)
            conv_state_p.append(u_p[:, t - keep:])
            full_s = jnp.concatenate([to_time_major(state_conv[l]), u_s], axis=0)
            conv_state_s.append(to_time_major(full_s[tn:]))
            xp = _conv_prompt(u_p, xp.reshape(b, t, d), *tail).reshape(b * t, d)
            xs = _conv_sample(full_s, xs.reshape(tn, bd, d), *tail).reshape(tn * bd, d)
            xp = _ffn(xp, *ffn_w, final_gain=last)
            xs = _ffn(xs, *ffn_w, final_gain=last)
        else:
            j = l - n_conv
            if kv_p is None:
                wkv = w_kv.astype(BF16)
                kv_p = _kv_proj(xp, kv_norm, wkv)
                kv_s = _kv_proj(xs, kv_norm, wkv)
                new_rows = [jnp.pad(to_time_major(a.reshape(tn, bd, hd)), ((0, 0), (0, 8 - tn), (0, 0)))
                            for a in kv_s[:2]]
            wq = w_q[j].astype(BF16)
            wo = w_o[j].astype(BF16)
            q_p = _q_proj(xp, norm_mix[l], wq).reshape(b, t, hd)
            q_s = to_time_major(_q_proj(xs, norm_mix[l], wq).reshape(tn, bd, hd))
            bias2 = b_logit[j] * LOG2_E
            a_p = _attn_prompt(q_p, kv_p[2].reshape(b, t, hd), kv_p[3].reshape(b, t, hd), bias2)
            a_s = _attn_sample(q_s, new_rows[0], new_rows[1], cache_k, cache_v, page_table, bias2)
            a_s = to_time_major(a_s).reshape(tn * bd, hd)
            xp = _ffn(xp, *ffn_w, attn=a_p.reshape(b * t, hd), wo=wo, final_gain=last)
            xs = _ffn(xs, *ffn_w, attn=a_s, wo=wo, final_gain=last)

    y_p = xp.reshape(b, t, d)
    y_s = to_time_major(xs.reshape(tn, bd, d))
    heads = lambda a, n0, n1: a.reshape(n0, n1, N_HEADS, HEAD_DIM)
    k_s = to_time_major(kv_s[0].reshape(tn, bd, hd))
    v_s = to_time_major(kv_s[1].reshape(tn, bd, hd))
    return (y_p, y_s, jnp.stack(conv_state_p, axis=0), jnp.stack(conv_state_s, axis=0),
            heads(kv_p[0], b, t), heads(kv_p[1], b, t), heads(k_s, bd, tn), heads(v_s, bd, tn))
```
